```python
import math
import jax, jax.numpy as jnp
from jax import lax
import numpy as np


D_MODEL = 1024
BATCH = 32
SEQ = 2048
DEPTH = 1
DEC_BATCH = 2
DEC_SEQ = 8192
PAST_LEN = 128

GRID_W = 64
RW_HEAD = 64
RW_HEADS = D_MODEL // RW_HEAD
RW_WIDTH = RW_HEADS * RW_HEAD
W_RANK = 64
A_RANK = 64
G_RANK = 128
DECAY_SCALE = math.exp(-0.5)
GN_EPS = 64e-5
ATT_HEAD = 128
ATT_Q_HEADS = D_MODEL // ATT_HEAD
ATT_KV_HEADS = 2
ATT_GROUP = ATT_Q_HEADS // ATT_KV_HEADS
ATT_Q = ATT_Q_HEADS * ATT_HEAD
ATT_KV = ATT_KV_HEADS * ATT_HEAD
Q_BLOCK = 128
ROPE_THETA = 10000.0
RMS_EPS = 1e-6
N_BRANCH = 2
D_FF = 4 * D_MODEL
LN_EPS = 1e-5
ALPHA = (2 * DEPTH) ** 0.25
BETA = (8 * DEPTH) ** -0.25
RW_SPLIT = [RW_WIDTH, RW_WIDTH, RW_WIDTH, 2 * W_RANK, 2 * A_RANK, G_RANK]
RW_COLS = sum(RW_SPLIT)
IN_SPLIT = [RW_COLS, ATT_Q, ATT_KV, ATT_KV, N_BRANCH * D_MODEL]
IN_COLS = sum(IN_SPLIT)

kernel_name = 'hybrid_rwkv7_axial_gqa_encoder'


def _offsets(sizes):
    out, acc = [], 0
    for s in sizes[:-1]:
        acc += s
        out.append(acc)
    return out


def _layernorm(x, g, b, eps):
    xf = x.astype(jnp.float32)
    mu = jnp.mean(xf, axis=-1, keepdims=True)
    var = jnp.mean(jnp.square(xf - mu), axis=-1, keepdims=True)
    return ((xf - mu) * lax.rsqrt(var + eps) * g + b).astype(x.dtype)


def _rmsnorm(x, g):
    xf = x.astype(jnp.float32)
    return (xf * lax.rsqrt(jnp.mean(xf * xf, axis=-1, keepdims=True) + RMS_EPS) * g).astype(x.dtype)


def _wkv_scan(r, w, kh, ak, v, kt, reverse):
    T, B, H, N = r.shape
    def step(S, inp):
        r_t, w_t, kh_t, ak_t, v_t, kt_t = inp
        sk = jnp.einsum('bhvk,bhk->bhv', S, kh_t)
        S = S * w_t[:, :, None, :] - sk[..., None] * ak_t[:, :, None, :] + v_t[..., None] * kt_t[:, :, None, :]
        return S, jnp.einsum('bhvk,bhk->bhv', S, r_t)
    S0 = jnp.zeros((B, H, N, N), jnp.float32)
    _, o = lax.scan(step, S0, (r, w, kh, ak, v, kt), reverse=reverse)
    return o


def _rwkv7_branch(z, mu_prev, mu_next, w0, w_up, a0, a_up, g_up, k_k, k_a, r_k, gn_g, gn_b):
    B, T, _ = z.shape
    H, N, C = RW_HEADS, RW_HEAD, RW_WIDTH
    f32 = jnp.float32
    z_prev = jnp.pad(z[:, :-1], ((0, 0), (1, 0), (0, 0)))
    z_next = jnp.pad(z[:, 1:], ((0, 0), (0, 1), (0, 0)))
    z = z + mu_prev * (z_prev - z) + mu_next * (z_next - z)
    r, k, v, wd, ad, gd = jnp.split(z, _offsets(RW_SPLIT), axis=-1)
    w_log = w0 + jnp.einsum('btdr,drc->btdc', jnp.tanh(wd.reshape(B, T, 2, W_RANK)), w_up)
    decay = jnp.exp(-DECAY_SCALE * jax.nn.sigmoid(w_log.astype(f32)))
    a = jax.nn.sigmoid((a0 + jnp.einsum('btdr,drc->btdc', ad.reshape(B, T, 2, A_RANK), a_up)).astype(f32))
    g = jnp.einsum('btr,rc->btc', jax.nn.sigmoid(gd), g_up)
    kf = k.astype(f32)
    kappa = (kf * k_k).reshape(B, T, 1, H, N)
    kappa_hat = kappa / jnp.maximum(jnp.sqrt(jnp.sum(kappa * kappa, axis=-1, keepdims=True)), 1e-12)
    k_tilde = (kf[:, :, None, :] * (1.0 + (a - 1.0) * k_a)).reshape(B, T, 2, H, N)
    ak = a.reshape(B, T, 2, H, N) * kappa_hat
    decay = decay.reshape(B, T, 2, H, N)
    r_h = r.astype(f32).reshape(B, T, H, N)
    v_h = v.astype(f32).reshape(B, T, H, N)
    tm = lambda u: jnp.moveaxis(u, 1, 0)
    r_t, v_t, kh_t = tm(r_h), tm(v_h), tm(kappa_hat[:, :, 0])
    o_f = _wkv_scan(r_t, tm(decay[:, :, 0]), kh_t, tm(ak[:, :, 0]), v_t, tm(k_tilde[:, :, 0]), False)
    o_b = _wkv_scan(r_t, tm(decay[:, :, 1]), kh_t, tm(ak[:, :, 1]), v_t, tm(k_tilde[:, :, 1]), True)
    o = jnp.moveaxis(o_f + o_b, 0, 1)
    o = _layernorm(o, gn_g.reshape(H, N), gn_b.reshape(H, N), GN_EPS)
    bonus = jnp.sum(r_h * jnp.sum(k_tilde, axis=2) * r_k.reshape(H, N), axis=-1, keepdims=True) * v_h
    return ((o + bonus).reshape(B, T, C) * g).astype(z.dtype)


def _axial_rope(T):
    rows = T // GRID_W
    row = jnp.repeat(jnp.arange(rows, dtype=jnp.float32), GRID_W)
    col = jnp.tile(jnp.arange(GRID_W, dtype=jnp.float32), rows)
    half = ATT_HEAD // 2
    inv = ROPE_THETA ** (-jnp.arange(0, half, 2, dtype=jnp.float32) / half)
    ang = jnp.stack([row[:, None] * inv, col[:, None] * inv], axis=1)
    ang = jnp.broadcast_to(ang[:, :, None, :], (T, 2, 2, half // 2)).reshape(T, ATT_HEAD)
    return jnp.cos(ang), jnp.sin(ang)


def _apply_rope(x, cos, sin):
    xs = x.reshape(x.shape[:-1] + (2, 2, ATT_HEAD // 4))
    rot = jnp.stack([-xs[..., 1, :], xs[..., 0, :]], axis=-2).reshape(x.shape)
    return (x * cos[None, :, None, :] + rot * sin[None, :, None, :]).astype(x.dtype)


def _attention_branch(q, k, v, q_norm, k_norm):
    B, T, _ = q.shape
    q = _rmsnorm(q.reshape(B, T, ATT_Q_HEADS, ATT_HEAD), q_norm)
    k = _rmsnorm(k.reshape(B, T, ATT_KV_HEADS, ATT_HEAD), k_norm)
    v = v.reshape(B, T, ATT_KV_HEADS, ATT_HEAD)
    cos, sin = _axial_rope(T)
    q = _apply_rope(q, cos, sin)
    k = _apply_rope(k, cos, sin)
    qb = q.reshape(B, T // Q_BLOCK, Q_BLOCK, ATT_KV_HEADS, ATT_GROUP, ATT_HEAD)
    qb = jnp.moveaxis(qb, 1, 0)
    scale = ATT_HEAD ** -0.5
    def block(qi):
        s = jnp.einsum('bqhgd,bkhd->bhgqk', qi, k).astype(jnp.float32) * scale
        p = jax.nn.softmax(s, axis=-1).astype(v.dtype)
        return jnp.einsum('bhgqk,bkhd->bqhgd', p, v)
    o = lax.map(block, qb)
    return jnp.moveaxis(o, 0, 1).reshape(B, T, ATT_Q)


def _encoder_layer(x, w_in, rw_mu_prev, rw_mu_next, rw_w0, rw_w_up, rw_a0, rw_a_up, rw_g_up,
                   rw_k_k, rw_k_a, rw_r_k, rw_gn_g, rw_gn_b, q_norm, k_norm,
                   w_proj_rwkv, w_proj_attn, w_out, ln1_g, ln1_b, w_ff1, w_ff2, ln2_g, ln2_b):
    B, T, D = x.shape
    z = jnp.einsum('btd,dc->btc', x, w_in)
    z_rw, q, k, v, gate_logits = jnp.split(z, _offsets(IN_SPLIT), axis=-1)
    o_rw = _rwkv7_branch(z_rw, rw_mu_prev, rw_mu_next, rw_w0, rw_w_up, rw_a0, rw_a_up, rw_g_up,
                         rw_k_k, rw_k_a, rw_r_k, rw_gn_g, rw_gn_b)
    o_att = _attention_branch(q, k, v, q_norm, k_norm)
    gates = jax.nn.sigmoid(gate_logits.astype(jnp.float32)).astype(x.dtype).reshape(B, T, N_BRANCH, D)
    merged = (gates[:, :, 0] * jnp.einsum('btc,cd->btd', o_rw, w_proj_rwkv)
              + gates[:, :, 1] * jnp.einsum('btc,cd->btd', o_att, w_proj_attn))
    mix = jnp.einsum('btd,de->bte', merged, w_out)
    h = _layernorm(ALPHA * x + mix, ln1_g, ln1_b, LN_EPS)
    ff = jnp.einsum('btf,fd->btd', jnp.square(jax.nn.relu(jnp.einsum('btd,df->btf', h, w_ff1))), w_ff2)
    return _layernorm(ALPHA * h + ff, ln2_g, ln2_b, LN_EPS)


def _trunk(x, params):
    h = x
    for l in range(DEPTH):
        h = _encoder_layer(h, *[p[l] for p in params])
    return h


def setup_inputs(seed: int = 0) -> dict:
    key = jax.random.key(seed)
    ks = iter(jax.random.split(key, 40))
    f32 = jnp.float32
    def nrm(shape, scale):
        return jax.random.normal(next(ks), shape, f32) * scale
    def unif(shape, lo, hi):
        return jax.random.uniform(next(ks), shape, f32, lo, hi)
    L, C, D = DEPTH, RW_WIDTH, D_MODEL
    return {
        'x_prompt': nrm((BATCH, SEQ, D), 1.0),
        'x_sample': nrm((DEC_BATCH, DEC_SEQ, D), 1.0),
        'w_in': nrm((L, D, IN_COLS), D ** -0.5),
        'rw_mu_prev': unif((L, RW_COLS), 0.0, 0.5),
        'rw_mu_next': unif((L, RW_COLS), 0.0, 0.5),
        'rw_w0': nrm((L, 2, C), 0.5) - 1.0,
        'rw_w_up': nrm((L, 2, W_RANK, C), 0.5 * W_RANK ** -0.5),
        'rw_a0': nrm((L, 2, C), 0.5),
        'rw_a_up': nrm((L, 2, A_RANK, C), 0.5 * A_RANK ** -0.5),
        'rw_g_up': nrm((L, G_RANK, C), G_RANK ** -0.5),
        'rw_k_k': 0.85 + nrm((L, C), 0.1),
        'rw_k_a': 1.0 + nrm((L, C), 0.1),
        'rw_r_k': nrm((L, C), 0.1),
        'rw_gn_g': 1.0 + nrm((L, C), 0.1),
        'rw_gn_b': nrm((L, C), 0.01),
        'q_norm': 1.0 + nrm((L, ATT_HEAD), 0.1),
        'k_norm': 1.0 + nrm((L, ATT_HEAD), 0.1),
        'w_proj_rwkv': nrm((L, C, D), BETA * C ** -0.5),
        'w_proj_attn': nrm((L, ATT_Q, D), BETA * ATT_Q ** -0.5),
        'w_out': nrm((L, D, D), BETA * D ** -0.5),
        'ln1_g': 1.0 + nrm((L, D), 0.1),
        'ln1_b': nrm((L, D), 0.01),
        'w_ff1': nrm((L, D, D_FF), BETA * D ** -0.5),
        'w_ff2': nrm((L, D_FF, D), BETA * D_FF ** -0.5),
        'ln2_g': 1.0 + nrm((L, D), 0.1),
        'ln2_b': nrm((L, D), 0.01),
    }


def reference(x_prompt, x_sample, w_in, rw_mu_prev, rw_mu_next, rw_w0, rw_w_up, rw_a0, rw_a_up,
              rw_g_up, rw_k_k, rw_k_a, rw_r_k, rw_gn_g, rw_gn_b, q_norm, k_norm,
              w_proj_rwkv, w_proj_attn, w_out, ln1_g, ln1_b, w_ff1, w_ff2, ln2_g, ln2_b):
    params = (w_in, rw_mu_prev, rw_mu_next, rw_w0, rw_w_up, rw_a0, rw_a_up, rw_g_up,
              rw_k_k, rw_k_a, rw_r_k, rw_gn_g, rw_gn_b, q_norm, k_norm,
              w_proj_rwkv, w_proj_attn, w_out, ln1_g, ln1_b, w_ff1, w_ff2, ln2_g, ln2_b)
    y_prompt = _trunk(x_prompt, params)
    y_sample = _trunk(x_sample, params)
    return (y_prompt, y_sample)
```

```python
import functools
import math

import jax
import jax.numpy as jnp
from jax import lax
from jax.experimental import pallas as pl
from jax.experimental.pallas import tpu as pltpu

F32 = jnp.float32
BF16 = jnp.bfloat16

D_MODEL = 1024
GRID_W = 64
RW_HEAD = 64
RW_WIDTH = 1024
W_RANK = 64
A_RANK = 64
G_RANK = 128
DECAY_SCALE = math.exp(-0.5)
GN_EPS = 64e-5
ATT_HEAD = 128
ATT_Q_HEADS = 8
ATT_KV_HEADS = 2
ATT_GROUP = ATT_Q_HEADS // ATT_KV_HEADS
ATT_Q = ATT_Q_HEADS * ATT_HEAD
ATT_KV = ATT_KV_HEADS * ATT_HEAD
ROPE_THETA = 10000.0
RMS_EPS = 1e-6
D_FF = 4 * D_MODEL
LN_EPS = 1e-5
DEPTH = 1
ALPHA = (2 * DEPTH) ** 0.25
RW_COLS = 3 * RW_WIDTH + 2 * W_RANK + 2 * A_RANK + G_RANK
QKV_COLS = ATT_Q + 2 * ATT_KV
GATE_COLS = 2 * D_MODEL

CHUNK = 64
PAIR = 2 * RW_HEAD
N_PAIR = RW_WIDTH // PAIR
LANES = 128
SUBLANES = 8
VMEM_LIMIT = 56 * 1024 * 1024


def _params(sem):
    return pltpu.CompilerParams(dimension_semantics=sem, vmem_limit_bytes=VMEM_LIMIT)


def _dot(a, b):
    return jnp.dot(a.astype(BF16), b.astype(BF16), preferred_element_type=F32)


def _dot_nt(a, b):
    return lax.dot_general(a.astype(BF16), b.astype(BF16), (((1,), (1,)), ((), ())),
                           preferred_element_type=F32)


def _split(x):
    hi = x.astype(BF16)
    lo = (x - hi.astype(F32)).astype(BF16)
    return hi, lo


def _dot_split_lhs(x, w):
    hi, lo = _split(x)
    return (jnp.dot(hi, w, preferred_element_type=F32) + jnp.dot(lo, w, preferred_element_type=F32))


def _dot_split_rhs(w, x):
    hi, lo = _split(x)
    return (jnp.dot(w, hi, preferred_element_type=F32) + jnp.dot(w, lo, preferred_element_type=F32))


def _sigmoid(x):
    return 1.0 / (1.0 + jnp.exp(-x))


def _resident(shape):
    nd = len(shape)
    return pl.BlockSpec(shape, lambda *_: (0,) * nd, pipeline_mode=pl.Buffered(1))


def _seg_ones():
    r = lax.broadcasted_iota(jnp.int32, (LANES, LANES), 0) // RW_HEAD
    c = lax.broadcasted_iota(jnp.int32, (LANES, LANES), 1) // RW_HEAD
    return jnp.where(r == c, 1.0, 0.0).astype(BF16)


def _segsum64(x, ones_bd):
    parts = [_dot_split_lhs(x[:, j * LANES:(j + 1) * LANES], ones_bd) for j in range(x.shape[1] // LANES)]
    return jnp.concatenate(parts, axis=1)


def _inproj_body(x_ref, wrw_ref, wqkv_ref, wg_ref, zrw_ref, qkv_ref, gate_ref):
    x = x_ref[...].astype(BF16)
    zrw_ref[...] = jnp.dot(x, wrw_ref[...], preferred_element_type=F32)
    qkv_ref[...] = jnp.dot(x, wqkv_ref[...], preferred_element_type=F32)
    gate_ref[...] = _sigmoid(jnp.dot(x, wg_ref[...], preferred_element_type=F32))


def _inproj(x, w_rw, w_qkv, w_gate, tm):
    n = x.shape[0]
    return pl.pallas_call(
        _inproj_body,
        grid=(n // tm,),
        in_specs=[pl.BlockSpec((tm, D_MODEL), lambda i: (i, 0)),
                  _resident((D_MODEL, RW_COLS)), _resident((D_MODEL, QKV_COLS)), _resident((D_MODEL, GATE_COLS))],
        out_specs=[pl.BlockSpec((tm, RW_COLS), lambda i: (i, 0)),
                   pl.BlockSpec((tm, QKV_COLS), lambda i: (i, 0)),
                   pl.BlockSpec((tm, GATE_COLS), lambda i: (i, 0))],
        out_shape=[jax.ShapeDtypeStruct((n, RW_COLS), F32),
                   jax.ShapeDtypeStruct((n, QKV_COLS), F32),
                   jax.ShapeDtypeStruct((n, GATE_COLS), F32)],
        compiler_params=_params(("parallel",)),
    )(x, w_rw, w_qkv, w_gate)


def _block_diag(x, lane_lo):
    top = jnp.where(lane_lo, x, 0.0)
    bot = jnp.where(lane_lo, 0.0, x)
    return jnp.concatenate([top, bot], axis=0).astype(BF16)


def _pair_transpose(x, lane_lo):
    top = jnp.where(lane_lo, x, 0.0)
    bot = jnp.where(lane_lo, 0.0, x)
    xt = jnp.transpose(jnp.concatenate([top, bot], axis=0))
    return xt[:CHUNK] + xt[CHUNK:]


def _rwkv_prep_body(zc_ref, zp_ref, zn_ref, mup_ref, mun_ref, w0_ref, wup_ref, a0_ref, aup_ref, gup_ref,
                    kk_ref, ka_ref, rk_ref,
                    phit_ref, qp_ref, psit_ref, o1_ref, bonus_ref, g_ref):
    c_idx = pl.program_id(1)
    n_chunks = pl.num_programs(1)
    L = CHUNK
    C = RW_WIDTH

    zc = zc_ref[0]
    row = lax.broadcasted_iota(jnp.int32, zc.shape, 0)
    prev_row = zp_ref[0][SUBLANES - 1:SUBLANES, :] * jnp.where(c_idx > 0, 1.0, 0.0)
    next_row = zn_ref[0][0:1, :] * jnp.where(c_idx < n_chunks - 1, 1.0, 0.0)
    z_prev = jnp.where(row == 0, prev_row, pltpu.roll(zc, 1, 0))
    z_next = jnp.where(row == L - 1, next_row, pltpu.roll(zc, L - 1, 0))
    z = zc + mup_ref[...] * (z_prev - zc) + mun_ref[...] * (z_next - zc)

    r = z[:, 0:C]
    k = z[:, C:2 * C]
    v = z[:, 2 * C:3 * C]
    o_wd = 3 * C
    tw = jnp.tanh(z[:, o_wd:o_wd + 2 * W_RANK]).astype(BF16)
    ad = z[:, o_wd + 2 * W_RANK:o_wd + 2 * W_RANK + 2 * A_RANK].astype(BF16)
    gd = _sigmoid(z[:, o_wd + 2 * W_RANK + 2 * A_RANK:RW_COLS]).astype(BF16)
    g_ref[0] = jnp.dot(gd, gup_ref[...], preferred_element_type=F32)

    ones_bd = _seg_ones()
    kappa = k * kk_ref[...]
    kh = kappa / jnp.maximum(jnp.sqrt(_segsum64(kappa * kappa, ones_bd)), 1e-12)

    ti = lax.broadcasted_iota(jnp.int32, (L, L), 0)
    si = lax.broadcasted_iota(jnp.int32, (L, L), 1)
    tp = lax.broadcasted_iota(jnp.int32, (L, PAIR), 0)
    lp = lax.broadcasted_iota(jnp.int32, (L, PAIR), 1)
    sp = lp % RW_HEAD
    lane_lo = lp < RW_HEAD
    eye_pair = jnp.where(sp == tp, 1.0, 0.0)

    kt_sum = jnp.zeros((L, C), F32)
    for d in range(2):
        if d == 0:
            tri = jnp.where(si <= ti, 1.0, 0.0).astype(BF16)
            m_strict, m_incl = sp < tp, sp <= tp
        else:
            tri = jnp.where(si >= ti, 1.0, 0.0).astype(BF16)
            m_strict, m_incl = sp > tp, sp >= tp
        wlog = w0_ref[d:d + 1, :] + jnp.dot(tw, wup_ref[d], preferred_element_type=F32)
        lw = -DECAY_SCALE * _sigmoid(wlog)
        a = _sigmoid(a0_ref[d:d + 1, :] + jnp.dot(ad, aup_ref[d], preferred_element_type=F32))
        kt = k * (1.0 + (a - 1.0) * ka_ref[...])
        kt_sum = kt_sum + kt
        ak = a * kh
        cum = _dot_split_rhs(tri, lw)
        ctot = cum[L - 1:L, :] if d == 0 else cum[0:1, :]
        excl = cum - lw
        a_t = -kh * jnp.exp(excl)
        q_t = r * jnp.exp(cum)
        inv = jnp.exp(-cum)
        b_t = ak * inv
        k_t = kt * inv
        suf = jnp.exp(ctot - cum)
        b_g = ak * suf
        k_g = kt * suf
        g_l = jnp.exp(ctot)

        for p in range(N_PAIR):
            sl = slice(p * PAIR, (p + 1) * PAIR)
            at_p, qt_p, v_p = a_t[:, sl], q_t[:, sl], v[:, sl]
            v_bd = _block_diag(v_p, lane_lo)
            aq = jnp.concatenate([at_p, qt_p], axis=0)
            gram_b = _dot_nt(aq, _block_diag(b_t[:, sl], lane_lo))
            gram_k = _dot_nt(aq, _block_diag(k_t[:, sl], lane_lo))
            m_ab = jnp.where(m_strict, gram_b[:L], 0.0)
            m_qb = jnp.where(m_incl, gram_b[L:], 0.0)
            m_ak = jnp.where(m_strict, gram_k[:L], 0.0)
            m_qk = jnp.where(m_incl, gram_k[L:], 0.0)
            pw = m_ab
            t_inv = eye_pair + m_ab
            for _ in range(5):
                pw = _dot(pw, _block_diag(pw, lane_lo))
                t_inv = t_inv + _dot(pw, _block_diag(t_inv, lane_lo))
            a_p = _dot(t_inv, _block_diag(at_p, lane_lo))
            u0 = _dot(t_inv, _block_diag(_dot(m_ak, v_bd), lane_lo))
            u0_bd = _block_diag(u0, lane_lo)
            ap_bd = _block_diag(a_p, lane_lo)
            q_p = qt_p + _dot(m_qb, ap_bd)
            o1 = _dot(m_qk, v_bd) + _dot(m_qb, u0_bd)
            bg_t = _pair_transpose(b_g[:, sl], lane_lo)
            kg_t = _pair_transpose(k_g[:, sl], lane_lo)
            phi_t = eye_pair * g_l[:, sl] + _dot(bg_t, ap_bd)
            psi_t = _dot(bg_t, u0_bd) + _dot(kg_t, v_bd)
            phit_ref[0, 0, d, p] = phi_t.astype(BF16)
            qp_ref[0, 0, d, p] = q_p.astype(BF16)
            psit_ref[0, 0, d, p] = psi_t
            o1_ref[0, 0, d, p] = o1

    bonus_ref[0] = _segsum64(r * kt_sum * rk_ref[...], ones_bd) * v


def _rwkv_prep(z_rw, mu_prev, mu_next, w0, w_up_pad, a0, a_up_pad, g_up, k_k, k_a, r_k):
    B, T, _ = z_rw.shape
    nc = T // CHUNK
    nb8 = T // SUBLANES
    blk8 = CHUNK // SUBLANES
    vec = lambda n: _resident((1, n))
    op_spec = pl.BlockSpec((1, 1, 2, N_PAIR, CHUNK, PAIR), lambda b, c: (b, c, 0, 0, 0, 0))
    tok_spec = pl.BlockSpec((1, CHUNK, RW_WIDTH), lambda b, c: (b, c, 0))
    op_shape = (B, nc, 2, N_PAIR, CHUNK, PAIR)
    return pl.pallas_call(
        _rwkv_prep_body,
        grid=(B, nc),
        in_specs=[pl.BlockSpec((1, CHUNK, RW_COLS), lambda b, c: (b, c, 0)),
                  pl.BlockSpec((1, SUBLANES, RW_COLS), lambda b, c: (b, jnp.maximum(c * blk8 - 1, 0), 0)),
                  pl.BlockSpec((1, SUBLANES, RW_COLS), lambda b, c: (b, jnp.minimum((c + 1) * blk8, nb8 - 1), 0)),
                  vec(RW_COLS), vec(RW_COLS),
                  _resident((2, RW_WIDTH)), _resident((2, 2 * W_RANK, RW_WIDTH)),
                  _resident((2, RW_WIDTH)), _resident((2, 2 * A_RANK, RW_WIDTH)),
                  _resident((G_RANK, RW_WIDTH)),
                  vec(RW_WIDTH), vec(RW_WIDTH), vec(RW_WIDTH)],
        out_specs=[op_spec, op_spec, op_spec, op_spec, tok_spec, tok_spec],
        out_shape=[jax.ShapeDtypeStruct(op_shape, BF16), jax.ShapeDtypeStruct(op_shape, BF16),
                   jax.ShapeDtypeStruct(op_shape, F32), jax.ShapeDtypeStruct(op_shape, F32),
                   jax.ShapeDtypeStruct((B, T, RW_WIDTH), F32), jax.ShapeDtypeStruct((B, T, RW_WIDTH), F32)],
        compiler_params=_params(("parallel", "parallel")),
    )(z_rw, z_rw, z_rw, mu_prev, mu_next, w0, w_up_pad, a0, a_up_pad, g_up, k_k, k_a, r_k)


def _rwkv_scan_body(phif_ref, qpf_ref, psif_ref, o1f_ref, phib_ref, qpb_ref, psib_ref, o1b_ref,
                    of_ref, ob_ref, st_ref):
    @pl.when(pl.program_id(1) == 0)
    def _():
        st_ref[...] = jnp.zeros_like(st_ref)

    lane_lo = lax.broadcasted_iota(jnp.int32, (CHUNK, PAIR), 1) < RW_HEAD
    dirs = ((phif_ref, qpf_ref, psif_ref, o1f_ref, of_ref), (phib_ref, qpb_ref, psib_ref, o1b_ref, ob_ref))
    for d, (phi_ref, qp_ref, psi_ref, o1_ref, out_ref) in enumerate(dirs):
        for p in range(N_PAIR):
            st = st_ref[d, p]
            st_hi = st.astype(BF16)
            st_lo = st - st_hi.astype(F32)
            lhs = jnp.concatenate([phi_ref[0, 0, 0, p], qp_ref[0, 0, 0, p]], axis=0)
            res = (jnp.dot(lhs, _block_diag(st_hi, lane_lo), preferred_element_type=F32)
                   + jnp.dot(lhs, _block_diag(st_lo, lane_lo), preferred_element_type=F32))
            st_ref[d, p] = res[:CHUNK] + psi_ref[0, 0, 0, p]
            out_ref[0, :, p * PAIR:(p + 1) * PAIR] = res[CHUNK:] + o1_ref[0, 0, 0, p]


def _rwkv_scan(phit, qp, psit, o1):
    B, nc = phit.shape[0], phit.shape[1]
    T = nc * CHUNK
    blk = (1, 1, 1, N_PAIR, CHUNK, PAIR)
    fwd = pl.BlockSpec(blk, lambda b, j: (b, j, 0, 0, 0, 0))
    bwd = pl.BlockSpec(blk, lambda b, j: (b, nc - 1 - j, 1, 0, 0, 0))
    return pl.pallas_call(
        _rwkv_scan_body,
        grid=(B, nc),
        in_specs=[fwd, fwd, fwd, fwd, bwd, bwd, bwd, bwd],
        out_specs=[pl.BlockSpec((1, CHUNK, RW_WIDTH), lambda b, j: (b, j, 0)),
                   pl.BlockSpec((1, CHUNK, RW_WIDTH), lambda b, j: (b, nc - 1 - j, 0))],
        out_shape=[jax.ShapeDtypeStruct((B, T, RW_WIDTH), F32), jax.ShapeDtypeStruct((B, T, RW_WIDTH), F32)],
        scratch_shapes=[pltpu.VMEM((2, N_PAIR, CHUNK, PAIR), F32)],
        compiler_params=_params(("parallel", "arbitrary")),
    )(phit, qp, psit, o1, phit, qp, psit, o1)


def _qk_prep_body(qkv_ref, cos_ref, sin_ref, qn_ref, kn_ref, q_ref, k_ref, v_ref):
    x = qkv_ref[0]
    cos = cos_ref[...]
    sin = sin_ref[...]
    lane = lax.broadcasted_iota(jnp.int32, cos.shape, 1)
    first = (lane % (ATT_HEAD // 2)) < (ATT_HEAD // 4)
    scale = ATT_HEAD ** -0.5

    def norm_rope(xh, gain):
        ms = jnp.mean(xh * xh, axis=-1, keepdims=True)
        xn = xh * lax.rsqrt(ms + RMS_EPS) * gain
        rot = jnp.where(first, -pltpu.roll(xn, ATT_HEAD - ATT_HEAD // 4, 1), pltpu.roll(xn, ATT_HEAD // 4, 1))
        return xn * cos + rot * sin

    for h in range(ATT_Q_HEADS):
        sl = slice(h * ATT_HEAD, (h + 1) * ATT_HEAD)
        q_ref[0, :, sl] = (norm_rope(x[:, sl], qn_ref[...]) * scale).astype(BF16)
    for h in range(ATT_KV_HEADS):
        sl = slice(h * ATT_HEAD, (h + 1) * ATT_HEAD)
        k_ref[0, :, sl] = norm_rope(x[:, ATT_Q + h * ATT_HEAD:ATT_Q + (h + 1) * ATT_HEAD], kn_ref[...]).astype(BF16)
    v_ref[0] = x[:, ATT_Q + ATT_KV:].astype(BF16)


def _qk_prep(qkv, cos, sin, q_norm, k_norm, tm):
    B, T, _ = qkv.shape
    return pl.pallas_call(
        _qk_prep_body,
        grid=(B, T // tm),
        in_specs=[pl.BlockSpec((1, tm, QKV_COLS), lambda b, i: (b, i, 0)),
                  pl.BlockSpec((tm, ATT_HEAD), lambda b, i: (i, 0)),
                  pl.BlockSpec((tm, ATT_HEAD), lambda b, i: (i, 0)),
                  _resident((1, ATT_HEAD)), _resident((1, ATT_HEAD))],
        out_specs=[pl.BlockSpec((1, tm, ATT_Q), lambda b, i: (b, i, 0)),
                   pl.BlockSpec((1, tm, ATT_KV), lambda b, i: (b, i, 0)),
                   pl.BlockSpec((1, tm, ATT_KV), lambda b, i: (b, i, 0))],
        out_shape=[jax.ShapeDtypeStruct((B, T, ATT_Q), BF16), jax.ShapeDtypeStruct((B, T, ATT_KV), BF16),
                   jax.ShapeDtypeStruct((B, T, ATT_KV), BF16)],
        compiler_params=_params(("parallel", "parallel")),
    )(qkv, cos, sin, q_norm, k_norm)


def _attn_body(q_ref, k_ref, v_ref, o_ref, m_ref, l_ref, acc_ref):
    ki = pl.program_id(3)

    @pl.when(ki == 0)
    def _():
        m_ref[...] = jnp.full_like(m_ref, -jnp.inf)
        l_ref[...] = jnp.zeros_like(l_ref)
        acc_ref[...] = jnp.zeros_like(acc_ref)

    q = q_ref[0]
    tq = q.shape[0]
    q4 = jnp.concatenate([q[:, g * ATT_HEAD:(g + 1) * ATT_HEAD] for g in range(ATT_GROUP)], axis=0)
    s = lax.dot_general(q4, k_ref[0], (((1,), (1,)), ((), ())), preferred_element_type=F32)
    m_prev = m_ref[...]
    m_cur = jnp.maximum(m_prev, jnp.max(s, axis=-1, keepdims=True))
    alpha = jnp.exp(m_prev - m_cur)
    p = jnp.exp(s - m_cur[:, 0:1])
    l_ref[...] = alpha * l_ref[...] + jnp.sum(p, axis=-1, keepdims=True)
    acc_ref[...] = alpha * acc_ref[...] + jnp.dot(p.astype(BF16), v_ref[0], preferred_element_type=F32)
    m_ref[...] = m_cur

    @pl.when(ki == pl.num_programs(3) - 1)
    def _():
        o = acc_ref[...] / l_ref[...]
        for g in range(ATT_GROUP):
            o_ref[0, :, g * ATT_HEAD:(g + 1) * ATT_HEAD] = o[g * tq:(g + 1) * tq].astype(o_ref.dtype)


def _attention(q, k, v, tq, tk):
    B, T, _ = q.shape
    gw = ATT_GROUP * ATT_HEAD
    return pl.pallas_call(
        _attn_body,
        grid=(B, ATT_KV_HEADS, T // tq, T // tk),
        in_specs=[pl.BlockSpec((1, tq, gw), lambda b, h, i, j: (b, i, h)),
                  pl.BlockSpec((1, tk, ATT_HEAD), lambda b, h, i, j: (b, j, h)),
                  pl.BlockSpec((1, tk, ATT_HEAD), lambda b, h, i, j: (b, j, h))],
        out_specs=pl.BlockSpec((1, tq, gw), lambda b, h, i, j: (b, i, h)),
        out_shape=jax.ShapeDtypeStruct((B, T, ATT_Q), BF16),
        scratch_shapes=[pltpu.VMEM((ATT_GROUP * tq, ATT_HEAD), F32),
                        pltpu.VMEM((ATT_GROUP * tq, ATT_HEAD), F32),
                        pltpu.VMEM((ATT_GROUP * tq, ATT_HEAD), F32)],
        compiler_params=_params(("parallel", "parallel", "parallel", "arbitrary")),
    )(q, k, v)


def _layernorm(y, g, b, eps):
    mu = jnp.mean(y, axis=-1, keepdims=True)
    d = y - mu
    var = jnp.mean(d * d, axis=-1, keepdims=True)
    return d * lax.rsqrt(var + eps) * g + b


def _merge_body(x_ref, of_ref, ob_ref, bonus_ref, g_ref, oatt_ref, gate_ref, gng_ref, gnb_ref,
                wprw_ref, wpatt_ref, wout_ref, ln1g_ref, ln1b_ref, h_ref):
    ones_bd = _seg_ones()
    o = of_ref[...] + ob_ref[...]
    inv_n = 1.0 / RW_HEAD
    mu = _segsum64(o, ones_bd) * inv_n
    d = o - mu
    var = _segsum64(d * d, ones_bd) * inv_n
    on = d * lax.rsqrt(var + GN_EPS) * gng_ref[...] + gnb_ref[...]
    o_rw = ((on + bonus_ref[...]) * g_ref[...]).astype(BF16)
    gates = gate_ref[...]
    merged = (gates[:, :D_MODEL] * jnp.dot(o_rw, wprw_ref[...], preferred_element_type=F32)
              + gates[:, D_MODEL:] * jnp.dot(oatt_ref[...], wpatt_ref[...], preferred_element_type=F32))
    mix = jnp.dot(merged.astype(BF16), wout_ref[...], preferred_element_type=F32)
    h_ref[...] = _layernorm(ALPHA * x_ref[...] + mix, ln1g_ref[...], ln1b_ref[...], LN_EPS)


def _merge(x, o_f, o_b, bonus, g, o_att, gates, gn_g, gn_b, w_prw, w_patt, w_out, ln1_g, ln1_b, tm):
    n = x.shape[0]
    tok = lambda w: pl.BlockSpec((tm, w), lambda i: (i, 0))
    vec = _resident((1, D_MODEL))
    mat = _resident((D_MODEL, D_MODEL))
    return pl.pallas_call(
        _merge_body,
        grid=(n // tm,),
        in_specs=[tok(D_MODEL), tok(D_MODEL), tok(D_MODEL), tok(D_MODEL), tok(D_MODEL), tok(D_MODEL),
                  tok(GATE_COLS), vec, vec, mat, mat, mat, vec, vec],
        out_specs=tok(D_MODEL),
        out_shape=jax.ShapeDtypeStruct((n, D_MODEL), F32),
        compiler_params=_params(("parallel",)),
    )(x, o_f, o_b, bonus, g, o_att, gates, gn_g, gn_b, w_prw, w_patt, w_out, ln1_g, ln1_b)


def _mlp_body(h_ref, w1_ref, w2_ref, g_ref, b_ref, o_ref):
    h = h_ref[...]
    u = jnp.maximum(jnp.dot(h.astype(BF16), w1_ref[...], preferred_element_type=F32), 0.0)
    ff = jnp.dot((u * u).astype(BF16), w2_ref[...], preferred_element_type=F32)
    o_ref[...] = _layernorm(ALPHA * h + ff, g_ref[...], b_ref[...], LN_EPS)


def _mlp(h, w1, w2, ln_g, ln_b, tm):
    n = h.shape[0]
    return pl.pallas_call(
        _mlp_body,
        grid=(n // tm,),
        in_specs=[pl.BlockSpec((tm, D_MODEL), lambda i: (i, 0)),
                  _resident((D_MODEL, D_FF)), _resident((D_FF, D_MODEL)),
                  _resident((1, D_MODEL)), _resident((1, D_MODEL))],
        out_specs=pl.BlockSpec((tm, D_MODEL), lambda i: (i, 0)),
        out_shape=jax.ShapeDtypeStruct((n, D_MODEL), F32),
        compiler_params=_params(("parallel",)),
    )(h, w1, w2, ln_g, ln_b)


def _axial_rope_tables(T):
    rows = T // GRID_W
    row = jnp.repeat(jnp.arange(rows, dtype=F32), GRID_W)
    col = jnp.tile(jnp.arange(GRID_W, dtype=F32), rows)
    half = ATT_HEAD // 2
    inv = ROPE_THETA ** (-jnp.arange(0, half, 2, dtype=F32) / half)
    ang = jnp.stack([row[:, None] * inv, col[:, None] * inv], axis=1)
    ang = jnp.broadcast_to(ang[:, :, None, :], (T, 2, 2, half // 2)).reshape(T, ATT_HEAD)
    return jnp.cos(ang), jnp.sin(ang)


def _pad_rank(w):
    z = jnp.zeros_like(w[0])
    return jnp.stack([jnp.concatenate([w[0], z], axis=0), jnp.concatenate([z, w[1]], axis=0)]).astype(BF16)


def _tile(n, pref):
    t = min(pref, n)
    while n % t:
        t //= 2
    return t


def _layer(x, p):
    B, T, D = x.shape
    n = B * T
    xf = x.reshape(n, D)
    z_rw, qkv, gates = _inproj(xf, p["w_rw"], p["w_qkv"], p["w_gate"], _tile(n, 256))

    phit, qp, psit, o1, bonus, g = _rwkv_prep(
        z_rw.reshape(B, T, RW_COLS), p["mu_prev"], p["mu_next"], p["w0"], p["w_up"], p["a0"], p["a_up"],
        p["g_up"], p["k_k"], p["k_a"], p["r_k"])
    o_f, o_b = _rwkv_scan(phit, qp, psit, o1)

    cos, sin = _axial_rope_tables(T)
    q_r, k_r, v_b = _qk_prep(qkv.reshape(B, T, QKV_COLS), cos, sin, p["q_norm"], p["k_norm"], _tile(T, 256))
    o_att = _attention(q_r, k_r, v_b, _tile(T, 256), _tile(T, 2048))

    h = _merge(xf, o_f.reshape(n, D), o_b.reshape(n, D), bonus.reshape(n, D), g.reshape(n, D),
               o_att.reshape(n, D), gates, p["gn_g"], p["gn_b"], p["w_prw"], p["w_patt"], p["w_out"],
               p["ln1_g"], p["ln1_b"], _tile(n, 256))
    y = _mlp(h, p["w_ff1"], p["w_ff2"], p["ln2_g"], p["ln2_b"], _tile(n, 512))
    return y.reshape(B, T, D)


def kernel(x_prompt, x_sample, w_in, rw_mu_prev, rw_mu_next, rw_w0, rw_w_up, rw_a0, rw_a_up, rw_g_up, rw_k_k,
           rw_k_a, rw_r_k, rw_gn_g, rw_gn_b, q_norm, k_norm, w_proj_rwkv, w_proj_attn, w_out, ln1_g, ln1_b,
           w_ff1, w_ff2, ln2_g, ln2_b):
    def layer_params(l):
        w = w_in[l].astype(BF16)
        row = lambda a: a[l].reshape(1, -1)
        return dict(
            w_rw=w[:, :RW_COLS], w_qkv=w[:, RW_COLS:RW_COLS + QKV_COLS], w_gate=w[:, RW_COLS + QKV_COLS:],
            mu_prev=row(rw_mu_prev), mu_next=row(rw_mu_next),
            w0=rw_w0[l], w_up=_pad_rank(rw_w_up[l]), a0=rw_a0[l], a_up=_pad_rank(rw_a_up[l]),
            g_up=rw_g_up[l].astype(BF16), k_k=row(rw_k_k), k_a=row(rw_k_a), r_k=row(rw_r_k),
            gn_g=row(rw_gn_g), gn_b=row(rw_gn_b), q_norm=row(q_norm), k_norm=row(k_norm),
            w_prw=w_proj_rwkv[l].astype(BF16), w_patt=w_proj_attn[l].astype(BF16), w_out=w_out[l].astype(BF16),
            ln1_g=row(ln1_g), ln1_b=row(ln1_b), w_ff1=w_ff1[l].astype(BF16), w_ff2=w_ff2[l].astype(BF16),
            ln2_g=row(ln2_g), ln2_b=row(ln2_b))

    layers = [layer_params(l) for l in range(w_in.shape[0])]

    def trunk(x):
        for p in layers:
            x = _layer(x, p)
        return x

    return trunk(x_prompt), trunk(x_sample)
```

```python
import functools
import math

import jax
import jax.numpy as jnp
from jax import lax
from jax.experimental import pallas as pl
from jax.experimental.pallas import tpu as pltpu

F32 = jnp.float32
BF16 = jnp.bfloat16

D_MODEL = 1024
GRID_W = 64
RW_HEAD = 64
RW_WIDTH = 1024
W_RANK = 64
A_RANK = 64
G_RANK = 128
DECAY_SCALE = math.exp(-0.5)
GN_EPS = 64e-5
ATT_HEAD = 128
ATT_Q_HEADS = 8
ATT_KV_HEADS = 2
ATT_GROUP = ATT_Q_HEADS // ATT_KV_HEADS
ATT_Q = ATT_Q_HEADS * ATT_HEAD
ATT_KV = ATT_KV_HEADS * ATT_HEAD
ROPE_THETA = 10000.0
RMS_EPS = 1e-6
D_FF = 4 * D_MODEL
LN_EPS = 1e-5
DEPTH = 1
ALPHA = (2 * DEPTH) ** 0.25
RW_COLS = 3 * RW_WIDTH + 2 * W_RANK + 2 * A_RANK + G_RANK
QKV_COLS = ATT_Q + 2 * ATT_KV
GATE_COLS = 2 * D_MODEL

CHUNK = 64
PAIR = 2 * RW_HEAD
N_PAIR = RW_WIDTH // PAIR
LANES = 128
SUBLANES = 8
VMEM_LIMIT = 56 * 1024 * 1024


def _params(sem):
    return pltpu.CompilerParams(dimension_semantics=sem, vmem_limit_bytes=VMEM_LIMIT)


def _dot(a, b):
    return jnp.dot(a.astype(BF16), b.astype(BF16), preferred_element_type=F32)


def _dot_nt(a, b):
    return lax.dot_general(a.astype(BF16), b.astype(BF16), (((1,), (1,)), ((), ())),
                           preferred_element_type=F32)


def _split(x):
    hi = x.astype(BF16)
    lo = (x - hi.astype(F32)).astype(BF16)
    return hi, lo


def _dot_split_lhs(x, w):
    hi, lo = _split(x)
    return (jnp.dot(hi, w, preferred_element_type=F32) + jnp.dot(lo, w, preferred_element_type=F32))


def _dot_split_rhs(w, x):
    hi, lo = _split(x)
    return (jnp.dot(w, hi, preferred_element_type=F32) + jnp.dot(w, lo, preferred_element_type=F32))


def _sigmoid(x):
    return 1.0 / (1.0 + jnp.exp(-x))


def _resident(shape):
    nd = len(shape)
    return pl.BlockSpec(shape, lambda *_: (0,) * nd, pipeline_mode=pl.Buffered(1))


def _seg_ones():
    r = lax.broadcasted_iota(jnp.int32, (LANES, LANES), 0) // RW_HEAD
    c = lax.broadcasted_iota(jnp.int32, (LANES, LANES), 1) // RW_HEAD
    return jnp.where(r == c, 1.0, 0.0).astype(BF16)


def _segsum64(x, ones_bd):
    parts = [_dot_split_lhs(x[:, j * LANES:(j + 1) * LANES], ones_bd) for j in range(x.shape[1] // LANES)]
    return jnp.concatenate(parts, axis=1)


def _inproj_body(x_ref, wrw_ref, wqkv_ref, wg_ref, zrw_ref, qkv_ref, gate_ref):
    x = x_ref[...].astype(BF16)
    zrw_ref[...] = jnp.dot(x, wrw_ref[...], preferred_element_type=F32)
    qkv_ref[...] = jnp.dot(x, wqkv_ref[...], preferred_element_type=F32)
    gate_ref[...] = _sigmoid(jnp.dot(x, wg_ref[...], preferred_element_type=F32))


def _inproj(x, w_rw, w_qkv, w_gate, tm):
    n = x.shape[0]
    return pl.pallas_call(
        _inproj_body,
        grid=(n // tm,),
        in_specs=[pl.BlockSpec((tm, D_MODEL), lambda i: (i, 0)),
                  _resident((D_MODEL, RW_COLS)), _resident((D_MODEL, QKV_COLS)), _resident((D_MODEL, GATE_COLS))],
        out_specs=[pl.BlockSpec((tm, RW_COLS), lambda i: (i, 0)),
                   pl.BlockSpec((tm, QKV_COLS), lambda i: (i, 0)),
                   pl.BlockSpec((tm, GATE_COLS), lambda i: (i, 0))],
        out_shape=[jax.ShapeDtypeStruct((n, RW_COLS), F32),
                   jax.ShapeDtypeStruct((n, QKV_COLS), F32),
                   jax.ShapeDtypeStruct((n, GATE_COLS), F32)],
        compiler_params=_params(("parallel",)),
    )(x, w_rw, w_qkv, w_gate)


def _block_diag(x, lane_lo):
    top = jnp.where(lane_lo, x, 0.0)
    bot = jnp.where(lane_lo, 0.0, x)
    return jnp.concatenate([top, bot], axis=0).astype(BF16)


def _pair_transpose(x, lane_lo):
    top = jnp.where(lane_lo, x, 0.0)
    bot = jnp.where(lane_lo, 0.0, x)
    xt = jnp.transpose(jnp.concatenate([top, bot], axis=0))
    return xt[:CHUNK] + xt[CHUNK:]


def _rwkv_prep_body(zc_ref, zp_ref, zn_ref, mup_ref, mun_ref, w0_ref, wup_ref, a0_ref, aup_ref, gup_ref,
                    kk_ref, ka_ref, rk_ref,
                    phit_ref, qp_ref, psit_ref, o1_ref, bonus_ref, g_ref):
    c_idx = pl.program_id(1)
    n_chunks = pl.num_programs(1)
    L = CHUNK
    C = RW_WIDTH

    zc = zc_ref[0]
    row = lax.broadcasted_iota(jnp.int32, zc.shape, 0)
    prev_row = zp_ref[0][SUBLANES - 1:SUBLANES, :] * jnp.where(c_idx > 0, 1.0, 0.0)
    next_row = zn_ref[0][0:1, :] * jnp.where(c_idx < n_chunks - 1, 1.0, 0.0)
    z_prev = jnp.where(row == 0, prev_row, pltpu.roll(zc, 1, 0))
    z_next = jnp.where(row == L - 1, next_row, pltpu.roll(zc, L - 1, 0))
    z = zc + mup_ref[...] * (z_prev - zc) + mun_ref[...] * (z_next - zc)

    r = z[:, 0:C]
    k = z[:, C:2 * C]
    v = z[:, 2 * C:3 * C]
    o_wd = 3 * C
    tw = jnp.tanh(z[:, o_wd:o_wd + 2 * W_RANK]).astype(BF16)
    ad = z[:, o_wd + 2 * W_RANK:o_wd + 2 * W_RANK + 2 * A_RANK].astype(BF16)
    gd = _sigmoid(z[:, o_wd + 2 * W_RANK + 2 * A_RANK:RW_COLS]).astype(BF16)
    g_ref[0] = jnp.dot(gd, gup_ref[...], preferred_element_type=F32)

    ones_bd = _seg_ones()
    kappa = k * kk_ref[...]
    kh = kappa / jnp.maximum(jnp.sqrt(_segsum64(kappa * kappa, ones_bd)), 1e-12)

    ti = lax.broadcasted_iota(jnp.int32, (L, L), 0)
    si = lax.broadcasted_iota(jnp.int32, (L, L), 1)
    tp = lax.broadcasted_iota(jnp.int32, (L, PAIR), 0)
    lp = lax.broadcasted_iota(jnp.int32, (L, PAIR), 1)
    sp = lp % RW_HEAD
    lane_lo = lp < RW_HEAD
    eye_pair = jnp.where(sp == tp, 1.0, 0.0)

    kt_sum = jnp.zeros((L, C), F32)
    chains = []
    for d in range(2):
        if d == 0:
            tri = jnp.where(si <= ti, 1.0, 0.0).astype(BF16)
            m_strict, m_incl = sp < tp, sp <= tp
        else:
            tri = jnp.where(si >= ti, 1.0, 0.0).astype(BF16)
            m_strict, m_incl = sp > tp, sp >= tp
        wlog = w0_ref[d:d + 1, :] + jnp.dot(tw, wup_ref[d], preferred_element_type=F32)
        lw = -DECAY_SCALE * _sigmoid(wlog)
        a = _sigmoid(a0_ref[d:d + 1, :] + jnp.dot(ad, aup_ref[d], preferred_element_type=F32))
        kt = k * (1.0 + (a - 1.0) * ka_ref[...])
        kt_sum = kt_sum + kt
        ak = a * kh
        cum = _dot_split_rhs(tri, lw)
        ctot = cum[L - 1:L, :] if d == 0 else cum[0:1, :]
        excl = cum - lw
        a_t = -kh * jnp.exp(excl)
        q_t = r * jnp.exp(cum)
        inv = jnp.exp(-cum)
        b_t = ak * inv
        k_t = kt * inv
        suf = jnp.exp(ctot - cum)
        b_g = ak * suf
        k_g = kt * suf
        g_l = jnp.exp(ctot)

        for p in range(N_PAIR):
            sl = slice(p * PAIR, (p + 1) * PAIR)
            chains.append(dict(d=d, p=p, at=a_t[:, sl], qt=q_t[:, sl], bt=b_t[:, sl], kt=k_t[:, sl],
                               bg=b_g[:, sl], kg=k_g[:, sl], v=v[:, sl], gl=g_l[:, sl],
                               ms=m_strict, mi=m_incl))

    bonus_ref[0] = _segsum64(r * kt_sum * rk_ref[...], ones_bd) * v

    bd = lambda x: _block_diag(x, lane_lo)
    bd2 = lambda x, y: jnp.concatenate([bd(x), bd(y)], axis=1)
    for ch in chains:
        aq = jnp.concatenate([ch["at"], ch["qt"]], axis=0)
        gram = _dot_nt(aq, jnp.concatenate([bd(ch["bt"]), bd(ch["kt"])], axis=0))
        ch["m_ab"] = jnp.where(ch["ms"], gram[:L, :PAIR], 0.0)
        ch["m_ak"] = jnp.where(ch["ms"], gram[:L, PAIR:], 0.0)
        ch["m_qb"] = jnp.where(ch["mi"], gram[L:, :PAIR], 0.0)
        ch["m_qk"] = jnp.where(ch["mi"], gram[L:, PAIR:], 0.0)
    for ch in chains:
        ch["bg_t"] = _pair_transpose(ch["bg"], lane_lo)
        ch["kg_t"] = _pair_transpose(ch["kg"], lane_lo)
    for ch in chains:
        ch["t"] = eye_pair + ch["m_ab"]
        ch["pw"] = _dot(ch["m_ab"], bd(ch["m_ab"]))
    for ch in chains:
        r3 = _dot(jnp.concatenate([ch["m_ak"], ch["m_qk"], ch["kg_t"]], axis=0), bd(ch["v"]))
        ch["p1"], ch["o1"], ch["psi"] = r3[:L], r3[L:2 * L], r3[2 * L:]
    for _ in range(4):
        for ch in chains:
            res = _dot(ch["pw"], bd2(ch["pw"], ch["t"]))
            ch["pw"] = res[:, :PAIR]
            ch["t"] = ch["t"] + res[:, PAIR:]
    for ch in chains:
        ch["t"] = ch["t"] + _dot(ch["pw"], bd(ch["t"]))
    for ch in chains:
        res = _dot(ch["t"], bd2(ch["at"], ch["p1"]))
        ch["a_p"], ch["u0"] = res[:, :PAIR], res[:, PAIR:]
    for ch in chains:
        res = _dot(jnp.concatenate([ch["m_qb"], ch["bg_t"]], axis=0), bd2(ch["a_p"], ch["u0"]))
        d, p = ch["d"], ch["p"]
        qp_ref[0, 0, d, p] = (ch["qt"] + res[:L, :PAIR]).astype(BF16)
        o1_ref[0, 0, d, p] = ch["o1"] + res[:L, PAIR:]
        phit_ref[0, 0, d, p] = (eye_pair * ch["gl"] + res[L:, :PAIR]).astype(BF16)
        psit_ref[0, 0, d, p] = ch["psi"] + res[L:, PAIR:]


def _rwkv_prep(z_rw, mu_prev, mu_next, w0, w_up_pad, a0, a_up_pad, g_up, k_k, k_a, r_k):
    B, T, _ = z_rw.shape
    nc = T // CHUNK
    nb8 = T // SUBLANES
    blk8 = CHUNK // SUBLANES
    vec = lambda n: _resident((1, n))
    op_spec = pl.BlockSpec((1, 1, 2, N_PAIR, CHUNK, PAIR), lambda b, c: (b, c, 0, 0, 0, 0))
    tok_spec = pl.BlockSpec((1, CHUNK, RW_WIDTH), lambda b, c: (b, c, 0))
    op_shape = (B, nc, 2, N_PAIR, CHUNK, PAIR)
    return pl.pallas_call(
        _rwkv_prep_body,
        grid=(B, nc),
        in_specs=[pl.BlockSpec((1, CHUNK, RW_COLS), lambda b, c: (b, c, 0)),
                  pl.BlockSpec((1, SUBLANES, RW_COLS), lambda b, c: (b, jnp.maximum(c * blk8 - 1, 0), 0)),
                  pl.BlockSpec((1, SUBLANES, RW_COLS), lambda b, c: (b, jnp.minimum((c + 1) * blk8, nb8 - 1), 0)),
                  vec(RW_COLS), vec(RW_COLS),
                  _resident((2, RW_WIDTH)), _resident((2, 2 * W_RANK, RW_WIDTH)),
                  _resident((2, RW_WIDTH)), _resident((2, 2 * A_RANK, RW_WIDTH)),
                  _resident((G_RANK, RW_WIDTH)),
                  vec(RW_WIDTH), vec(RW_WIDTH), vec(RW_WIDTH)],
        out_specs=[op_spec, op_spec, op_spec, op_spec, tok_spec, tok_spec],
        out_shape=[jax.ShapeDtypeStruct(op_shape, BF16), jax.ShapeDtypeStruct(op_shape, BF16),
                   jax.ShapeDtypeStruct(op_shape, F32), jax.ShapeDtypeStruct(op_shape, F32),
                   jax.ShapeDtypeStruct((B, T, RW_WIDTH), F32), jax.ShapeDtypeStruct((B, T, RW_WIDTH), F32)],
        compiler_params=_params(("parallel", "parallel")),
    )(z_rw, z_rw, z_rw, mu_prev, mu_next, w0, w_up_pad, a0, a_up_pad, g_up, k_k, k_a, r_k)


def _rwkv_scan_body(phif_ref, qpf_ref, psif_ref, o1f_ref, phib_ref, qpb_ref, psib_ref, o1b_ref,
                    of_ref, ob_ref, st_ref):
    @pl.when(pl.program_id(1) == 0)
    def _():
        st_ref[...] = jnp.zeros_like(st_ref)

    lane_lo = lax.broadcasted_iota(jnp.int32, (CHUNK, PAIR), 1) < RW_HEAD
    dirs = ((phif_ref, qpf_ref, psif_ref, o1f_ref, of_ref), (phib_ref, qpb_ref, psib_ref, o1b_ref, ob_ref))
    for d, (phi_ref, qp_ref, psi_ref, o1_ref, out_ref) in enumerate(dirs):
        for p in range(N_PAIR):
            st = st_ref[d, p]
            st_hi = st.astype(BF16)
            st_lo = st - st_hi.astype(F32)
            lhs = jnp.concatenate([phi_ref[0, 0, 0, p], qp_ref[0, 0, 0, p]], axis=0)
            res = (jnp.dot(lhs, _block_diag(st_hi, lane_lo), preferred_element_type=F32)
                   + jnp.dot(lhs, _block_diag(st_lo, lane_lo), preferred_element_type=F32))
            st_ref[d, p] = res[:CHUNK] + psi_ref[0, 0, 0, p]
            out_ref[0, :, p * PAIR:(p + 1) * PAIR] = res[CHUNK:] + o1_ref[0, 0, 0, p]


def _rwkv_scan(phit, qp, psit, o1):
    B, nc = phit.shape[0], phit.shape[1]
    T = nc * CHUNK
    blk = (1, 1, 1, N_PAIR, CHUNK, PAIR)
    fwd = pl.BlockSpec(blk, lambda b, j: (b, j, 0, 0, 0, 0))
    bwd = pl.BlockSpec(blk, lambda b, j: (b, nc - 1 - j, 1, 0, 0, 0))
    return pl.pallas_call(
        _rwkv_scan_body,
        grid=(B, nc),
        in_specs=[fwd, fwd, fwd, fwd, bwd, bwd, bwd, bwd],
        out_specs=[pl.BlockSpec((1, CHUNK, RW_WIDTH), lambda b, j: (b, j, 0)),
                   pl.BlockSpec((1, CHUNK, RW_WIDTH), lambda b, j: (b, nc - 1 - j, 0))],
        out_shape=[jax.ShapeDtypeStruct((B, T, RW_WIDTH), F32), jax.ShapeDtypeStruct((B, T, RW_WIDTH), F32)],
        scratch_shapes=[pltpu.VMEM((2, N_PAIR, CHUNK, PAIR), F32)],
        compiler_params=_params(("parallel", "arbitrary")),
    )(phit, qp, psit, o1, phit, qp, psit, o1)


def _qk_prep_body(qkv_ref, cos_ref, sin_ref, qn_ref, kn_ref, q_ref, k_ref, v_ref):
    x = qkv_ref[0]
    cos = cos_ref[...]
    sin = sin_ref[...]
    lane = lax.broadcasted_iota(jnp.int32, cos.shape, 1)
    first = (lane % (ATT_HEAD // 2)) < (ATT_HEAD // 4)
    scale = ATT_HEAD ** -0.5

    def norm_rope(xh, gain):
        ms = jnp.mean(xh * xh, axis=-1, keepdims=True)
        xn = xh * lax.rsqrt(ms + RMS_EPS) * gain
        rot = jnp.where(first, -pltpu.roll(xn, ATT_HEAD - ATT_HEAD // 4, 1), pltpu.roll(xn, ATT_HEAD // 4, 1))
        return xn * cos + rot * sin

    for h in range(ATT_Q_HEADS):
        sl = slice(h * ATT_HEAD, (h + 1) * ATT_HEAD)
        q_ref[0, :, sl] = (norm_rope(x[:, sl], qn_ref[...]) * scale).astype(BF16)
    for h in range(ATT_KV_HEADS):
        sl = slice(h * ATT_HEAD, (h + 1) * ATT_HEAD)
        k_ref[0, :, sl] = norm_rope(x[:, ATT_Q + h * ATT_HEAD:ATT_Q + (h + 1) * ATT_HEAD], kn_ref[...]).astype(BF16)
    v_ref[0] = x[:, ATT_Q + ATT_KV:].astype(BF16)


def _qk_prep(qkv, cos, sin, q_norm, k_norm, tm):
    B, T, _ = qkv.shape
    return pl.pallas_call(
        _qk_prep_body,
        grid=(B, T // tm),
        in_specs=[pl.BlockSpec((1, tm, QKV_COLS), lambda b, i: (b, i, 0)),
                  pl.BlockSpec((tm, ATT_HEAD), lambda b, i: (i, 0)),
                  pl.BlockSpec((tm, ATT_HEAD), lambda b, i: (i, 0)),
                  _resident((1, ATT_HEAD)), _resident((1, ATT_HEAD))],
        out_specs=[pl.BlockSpec((1, tm, ATT_Q), lambda b, i: (b, i, 0)),
                   pl.BlockSpec((1, tm, ATT_KV), lambda b, i: (b, i, 0)),
                   pl.BlockSpec((1, tm, ATT_KV), lambda b, i: (b, i, 0))],
        out_shape=[jax.ShapeDtypeStruct((B, T, ATT_Q), BF16), jax.ShapeDtypeStruct((B, T, ATT_KV), BF16),
                   jax.ShapeDtypeStruct((B, T, ATT_KV), BF16)],
        compiler_params=_params(("parallel", "parallel")),
    )(qkv, cos, sin, q_norm, k_norm)


def _attn_body(q_ref, k_ref, v_ref, o_ref, m_ref, l_ref, acc_ref):
    ki = pl.program_id(3)

    @pl.when(ki == 0)
    def _():
        m_ref[...] = jnp.full_like(m_ref, -jnp.inf)
        l_ref[...] = jnp.zeros_like(l_ref)
        acc_ref[...] = jnp.zeros_like(acc_ref)

    q = q_ref[0]
    tq = q.shape[0]
    q4 = jnp.concatenate([q[:, g * ATT_HEAD:(g + 1) * ATT_HEAD] for g in range(ATT_GROUP)], axis=0)
    s = lax.dot_general(q4, k_ref[0], (((1,), (1,)), ((), ())), preferred_element_type=F32)
    m_prev = m_ref[...]
    m_cur = jnp.maximum(m_prev, jnp.max(s, axis=-1, keepdims=True))
    alpha = jnp.exp(m_prev - m_cur)
    p = jnp.exp(s - m_cur[:, 0:1])
    l_ref[...] = alpha * l_ref[...] + jnp.sum(p, axis=-1, keepdims=True)
    acc_ref[...] = alpha * acc_ref[...] + jnp.dot(p.astype(BF16), v_ref[0], preferred_element_type=F32)
    m_ref[...] = m_cur

    @pl.when(ki == pl.num_programs(3) - 1)
    def _():
        o = acc_ref[...] / l_ref[...]
        for g in range(ATT_GROUP):
            o_ref[0, :, g * ATT_HEAD:(g + 1) * ATT_HEAD] = o[g * tq:(g + 1) * tq].astype(o_ref.dtype)


def _attention(q, k, v, tq, tk):
    B, T, _ = q.shape
    gw = ATT_GROUP * ATT_HEAD
    return pl.pallas_call(
        _attn_body,
        grid=(B, ATT_KV_HEADS, T // tq, T // tk),
        in_specs=[pl.BlockSpec((1, tq, gw), lambda b, h, i, j: (b, i, h)),
                  pl.BlockSpec((1, tk, ATT_HEAD), lambda b, h, i, j: (b, j, h)),
                  pl.BlockSpec((1, tk, ATT_HEAD), lambda b, h, i, j: (b, j, h))],
        out_specs=pl.BlockSpec((1, tq, gw), lambda b, h, i, j: (b, i, h)),
        out_shape=jax.ShapeDtypeStruct((B, T, ATT_Q), BF16),
        scratch_shapes=[pltpu.VMEM((ATT_GROUP * tq, ATT_HEAD), F32),
                        pltpu.VMEM((ATT_GROUP * tq, ATT_HEAD), F32),
                        pltpu.VMEM((ATT_GROUP * tq, ATT_HEAD), F32)],
        compiler_params=_params(("parallel", "parallel", "parallel", "arbitrary")),
    )(q, k, v)


def _layernorm(y, g, b, eps):
    mu = jnp.mean(y, axis=-1, keepdims=True)
    d = y - mu
    var = jnp.mean(d * d, axis=-1, keepdims=True)
    return d * lax.rsqrt(var + eps) * g + b


def _merge_body(x_ref, of_ref, ob_ref, bonus_ref, g_ref, oatt_ref, gate_ref, gng_ref, gnb_ref,
                wprw_ref, wpatt_ref, wout_ref, ln1g_ref, ln1b_ref, h_ref):
    ones_bd = _seg_ones()
    o = of_ref[...] + ob_ref[...]
    inv_n = 1.0 / RW_HEAD
    mu = _segsum64(o, ones_bd) * inv_n
    d = o - mu
    var = _segsum64(d * d, ones_bd) * inv_n
    on = d * lax.rsqrt(var + GN_EPS) * gng_ref[...] + gnb_ref[...]
    o_rw = ((on + bonus_ref[...]) * g_ref[...]).astype(BF16)
    gates = gate_ref[...]
    merged = (gates[:, :D_MODEL] * jnp.dot(o_rw, wprw_ref[...], preferred_element_type=F32)
              + gates[:, D_MODEL:] * jnp.dot(oatt_ref[...], wpatt_ref[...], preferred_element_type=F32))
    mix = jnp.dot(merged.astype(BF16), wout_ref[...], preferred_element_type=F32)
    h_ref[...] = _layernorm(ALPHA * x_ref[...] + mix, ln1g_ref[...], ln1b_ref[...], LN_EPS)


def _merge(x, o_f, o_b, bonus, g, o_att, gates, gn_g, gn_b, w_prw, w_patt, w_out, ln1_g, ln1_b, tm):
    n = x.shape[0]
    tok = lambda w: pl.BlockSpec((tm, w), lambda i: (i, 0))
    vec = _resident((1, D_MODEL))
    mat = _resident((D_MODEL, D_MODEL))
    return pl.pallas_call(
        _merge_body,
        grid=(n // tm,),
        in_specs=[tok(D_MODEL), tok(D_MODEL), tok(D_MODEL), tok(D_MODEL), tok(D_MODEL), tok(D_MODEL),
                  tok(GATE_COLS), vec, vec, mat, mat, mat, vec, vec],
        out_specs=tok(D_MODEL),
        out_shape=jax.ShapeDtypeStruct((n, D_MODEL), F32),
        compiler_params=_params(("parallel",)),
    )(x, o_f, o_b, bonus, g, o_att, gates, gn_g, gn_b, w_prw, w_patt, w_out, ln1_g, ln1_b)


def _mlp_body(h_ref, w1_ref, w2_ref, g_ref, b_ref, o_ref):
    h = h_ref[...]
    u = jnp.maximum(jnp.dot(h.astype(BF16), w1_ref[...], preferred_element_type=F32), 0.0)
    ff = jnp.dot((u * u).astype(BF16), w2_ref[...], preferred_element_type=F32)
    o_ref[...] = _layernorm(ALPHA * h + ff, g_ref[...], b_ref[...], LN_EPS)


def _mlp(h, w1, w2, ln_g, ln_b, tm):
    n = h.shape[0]
    return pl.pallas_call(
        _mlp_body,
        grid=(n // tm,),
        in_specs=[pl.BlockSpec((tm, D_MODEL), lambda i: (i, 0)),
                  _resident((D_MODEL, D_FF)), _resident((D_FF, D_MODEL)),
                  _resident((1, D_MODEL)), _resident((1, D_MODEL))],
        out_specs=pl.BlockSpec((tm, D_MODEL), lambda i: (i, 0)),
        out_shape=jax.ShapeDtypeStruct((n, D_MODEL), F32),
        compiler_params=_params(("parallel",)),
    )(h, w1, w2, ln_g, ln_b)


def _axial_rope_tables(T):
    rows = T // GRID_W
    row = jnp.repeat(jnp.arange(rows, dtype=F32), GRID_W)
    col = jnp.tile(jnp.arange(GRID_W, dtype=F32), rows)
    half = ATT_HEAD // 2
    inv = ROPE_THETA ** (-jnp.arange(0, half, 2, dtype=F32) / half)
    ang = jnp.stack([row[:, None] * inv, col[:, None] * inv], axis=1)
    ang = jnp.broadcast_to(ang[:, :, None, :], (T, 2, 2, half // 2)).reshape(T, ATT_HEAD)
    return jnp.cos(ang), jnp.sin(ang)


def _pad_rank(w):
    z = jnp.zeros_like(w[0])
    return jnp.stack([jnp.concatenate([w[0], z], axis=0), jnp.concatenate([z, w[1]], axis=0)]).astype(BF16)


def _tile(n, pref):
    t = min(pref, n)
    while n % t:
        t //= 2
    return t


def _layer(x, p):
    B, T, D = x.shape
    n = B * T
    xf = x.reshape(n, D)
    z_rw, qkv, gates = _inproj(xf, p["w_rw"], p["w_qkv"], p["w_gate"], _tile(n, 256))

    phit, qp, psit, o1, bonus, g = _rwkv_prep(
        z_rw.reshape(B, T, RW_COLS), p["mu_prev"], p["mu_next"], p["w0"], p["w_up"], p["a0"], p["a_up"],
        p["g_up"], p["k_k"], p["k_a"], p["r_k"])
    o_f, o_b = _rwkv_scan(phit, qp, psit, o1)

    cos, sin = _axial_rope_tables(T)
    q_r, k_r, v_b = _qk_prep(qkv.reshape(B, T, QKV_COLS), cos, sin, p["q_norm"], p["k_norm"], _tile(T, 256))
    o_att = _attention(q_r, k_r, v_b, _tile(T, 256), _tile(T, 2048))

    h = _merge(xf, o_f.reshape(n, D), o_b.reshape(n, D), bonus.reshape(n, D), g.reshape(n, D),
               o_att.reshape(n, D), gates, p["gn_g"], p["gn_b"], p["w_prw"], p["w_patt"], p["w_out"],
               p["ln1_g"], p["ln1_b"], _tile(n, 256))
    y = _mlp(h, p["w_ff1"], p["w_ff2"], p["ln2_g"], p["ln2_b"], _tile(n, 512))
    return y.reshape(B, T, D)


def kernel(x_prompt, x_sample, w_in, rw_mu_prev, rw_mu_next, rw_w0, rw_w_up, rw_a0, rw_a_up, rw_g_up, rw_k_k,
           rw_k_a, rw_r_k, rw_gn_g, rw_gn_b, q_norm, k_norm, w_proj_rwkv, w_proj_attn, w_out, ln1_g, ln1_b,
           w_ff1, w_ff2, ln2_g, ln2_b):
    def layer_params(l):
        w = w_in[l].astype(BF16)
        row = lambda a: a[l].reshape(1, -1)
        return dict(
            w_rw=w[:, :RW_COLS], w_qkv=w[:, RW_COLS:RW_COLS + QKV_COLS], w_gate=w[:, RW_COLS + QKV_COLS:],
            mu_prev=row(rw_mu_prev), mu_next=row(rw_mu_next),
            w0=rw_w0[l], w_up=_pad_rank(rw_w_up[l]), a0=rw_a0[l], a_up=_pad_rank(rw_a_up[l]),
            g_up=rw_g_up[l].astype(BF16), k_k=row(rw_k_k), k_a=row(rw_k_a), r_k=row(rw_r_k),
            gn_g=row(rw_gn_g), gn_b=row(rw_gn_b), q_norm=row(q_norm), k_norm=row(k_norm),
            w_prw=w_proj_rwkv[l].astype(BF16), w_patt=w_proj_attn[l].astype(BF16), w_out=w_out[l].astype(BF16),
            ln1_g=row(ln1_g), ln1_b=row(ln1_b), w_ff1=w_ff1[l].astype(BF16), w_ff2=w_ff2[l].astype(BF16),
            ln2_g=row(ln2_g), ln2_b=row(ln2_b))

    layers = [layer_params(l) for l in range(w_in.shape[0])]

    def trunk(x):
        for p in layers:
            x = _layer(x, p)
        return x

    return trunk(x_prompt), trunk(x_sample)
```

```python
import functools
import math

import jax
import jax.numpy as jnp
from jax import lax
from jax.experimental import pallas as pl
from jax.experimental.pallas import tpu as pltpu

F32 = jnp.float32
BF16 = jnp.bfloat16

D_MODEL = 1024
GRID_W = 64
RW_HEAD = 64
RW_WIDTH = 1024
W_RANK = 64
A_RANK = 64
G_RANK = 128
DECAY_SCALE = math.exp(-0.5)
GN_EPS = 64e-5
ATT_HEAD = 128
ATT_Q_HEADS = 8
ATT_KV_HEADS = 2
ATT_GROUP = ATT_Q_HEADS // ATT_KV_HEADS
ATT_Q = ATT_Q_HEADS * ATT_HEAD
ATT_KV = ATT_KV_HEADS * ATT_HEAD
ROPE_THETA = 10000.0
RMS_EPS = 1e-6
Q_SCALE = ATT_HEAD ** -0.5 * math.log2(math.e)
D_FF = 4 * D_MODEL
LN_EPS = 1e-5
DEPTH = 1
ALPHA = (2 * DEPTH) ** 0.25
RW_COLS = 3 * RW_WIDTH + 2 * W_RANK + 2 * A_RANK + G_RANK
QKV_COLS = ATT_Q + 2 * ATT_KV
GATE_COLS = 2 * D_MODEL

N_ATT_PASS = 1
CHUNK = 64
PAIR = 2 * RW_HEAD
N_PAIR = RW_WIDTH // PAIR
LANES = 128
SUBLANES = 8
VMEM_LIMIT = 56 * 1024 * 1024


def _params(sem):
    return pltpu.CompilerParams(dimension_semantics=sem, vmem_limit_bytes=VMEM_LIMIT)


def _dot(a, b):
    return jnp.dot(a.astype(BF16), b.astype(BF16), preferred_element_type=F32)


def _dot_nt(a, b):
    return lax.dot_general(a.astype(BF16), b.astype(BF16), (((1,), (1,)), ((), ())),
                           preferred_element_type=F32)


def _split(x):
    hi = x.astype(BF16)
    lo = (x - hi.astype(F32)).astype(BF16)
    return hi, lo


def _dot_split_lhs(x, w):
    hi, lo = _split(x)
    return (jnp.dot(hi, w, preferred_element_type=F32) + jnp.dot(lo, w, preferred_element_type=F32))


def _dot_split_rhs(w, x):
    hi, lo = _split(x)
    return (jnp.dot(w, hi, preferred_element_type=F32) + jnp.dot(w, lo, preferred_element_type=F32))


def _sigmoid(x):
    return 1.0 / (1.0 + jnp.exp(-x))


def _resident(shape):
    nd = len(shape)
    return pl.BlockSpec(shape, lambda *_: (0,) * nd, pipeline_mode=pl.Buffered(1))


def _seg_ones():
    r = lax.broadcasted_iota(jnp.int32, (LANES, LANES), 0) // RW_HEAD
    c = lax.broadcasted_iota(jnp.int32, (LANES, LANES), 1) // RW_HEAD
    return jnp.where(r == c, 1.0, 0.0).astype(BF16)


def _segsum64(x, ones_bd):
    parts = [_dot_split_lhs(x[:, j * LANES:(j + 1) * LANES], ones_bd) for j in range(x.shape[1] // LANES)]
    return jnp.concatenate(parts, axis=1)


def _norm_rope(xh, gain, cos, sin, first):
    ms = jnp.mean(xh * xh, axis=-1, keepdims=True)
    xn = xh * lax.rsqrt(ms + RMS_EPS) * gain
    rot = jnp.where(first, -pltpu.roll(xn, ATT_HEAD - ATT_HEAD // 4, 1), pltpu.roll(xn, ATT_HEAD // 4, 1))
    return xn * cos + rot * sin


def _inproj_body(x_ref, wrw_ref, wqkv_ref, wg_ref, cos_ref, sin_ref, qn_ref, kn_ref,
                 zrw_ref, q_ref, k_ref, v_ref, gate_ref):
    x = x_ref[...].astype(BF16)
    zrw_ref[...] = jnp.dot(x, wrw_ref[...], preferred_element_type=F32)
    gate_ref[...] = _sigmoid(jnp.dot(x, wg_ref[...], preferred_element_type=F32)).astype(BF16)
    qkv = jnp.dot(x, wqkv_ref[...], preferred_element_type=F32)
    cos = cos_ref[...]
    sin = sin_ref[...]
    lane = lax.broadcasted_iota(jnp.int32, cos.shape, 1)
    first = (lane % (ATT_HEAD // 2)) < (ATT_HEAD // 4)
    for h in range(ATT_Q_HEADS):
        sl = slice(h * ATT_HEAD, (h + 1) * ATT_HEAD)
        q_ref[:, sl] = (_norm_rope(qkv[:, sl], qn_ref[...], cos, sin, first) * Q_SCALE).astype(BF16)
    for h in range(ATT_KV_HEADS):
        sl = slice(h * ATT_HEAD, (h + 1) * ATT_HEAD)
        k_ref[:, sl] = _norm_rope(qkv[:, ATT_Q + h * ATT_HEAD:ATT_Q + (h + 1) * ATT_HEAD], kn_ref[...],
                                  cos, sin, first).astype(BF16)
    v_ref[...] = qkv[:, ATT_Q + ATT_KV:].astype(BF16)


def _inproj(x, w_rw, w_qkv, w_gate, cos, sin, q_norm, k_norm, tm, seq_len):
    n = x.shape[0]
    tiles_per_seq = seq_len // tm
    tok = lambda w: pl.BlockSpec((tm, w), lambda i: (i, 0))
    rope = pl.BlockSpec((tm, ATT_HEAD), lambda i: (i % tiles_per_seq, 0))
    return pl.pallas_call(
        _inproj_body,
        grid=(n // tm,),
        in_specs=[tok(D_MODEL),
                  _resident((D_MODEL, RW_COLS)), _resident((D_MODEL, QKV_COLS)), _resident((D_MODEL, GATE_COLS)),
                  rope, rope, _resident((1, ATT_HEAD)), _resident((1, ATT_HEAD))],
        out_specs=[tok(RW_COLS), tok(ATT_Q), tok(ATT_KV), tok(ATT_KV), tok(GATE_COLS)],
        out_shape=[jax.ShapeDtypeStruct((n, RW_COLS), F32),
                   jax.ShapeDtypeStruct((n, ATT_Q), BF16),
                   jax.ShapeDtypeStruct((n, ATT_KV), BF16),
                   jax.ShapeDtypeStruct((n, ATT_KV), BF16),
                   jax.ShapeDtypeStruct((n, GATE_COLS), BF16)],
        compiler_params=_params(("parallel",)),
    )(x, w_rw, w_qkv, w_gate, cos, sin, q_norm, k_norm)


def _block_diag(x, lane_lo):
    top = jnp.where(lane_lo, x, 0.0)
    bot = jnp.where(lane_lo, 0.0, x)
    return jnp.concatenate([top, bot], axis=0).astype(BF16)


def _pair_transpose(x, lane_lo):
    top = jnp.where(lane_lo, x, 0.0)
    bot = jnp.where(lane_lo, 0.0, x)
    xt = jnp.transpose(jnp.concatenate([top, bot], axis=0))
    return xt[:CHUNK] + xt[CHUNK:]


def _rwkv_prep_body(zc_ref, zp_ref, zn_ref, mup_ref, mun_ref, w0_ref, wup_ref, a0_ref, aup_ref, gup_ref,
                    kk_ref, ka_ref, rk_ref,
                    phit_ref, qp_ref, psit_ref, o1_ref, bonus_ref, g_ref):
    c_idx = pl.program_id(1)
    n_chunks = pl.num_programs(1)
    L = CHUNK
    C = RW_WIDTH

    zc = zc_ref[0]
    row = lax.broadcasted_iota(jnp.int32, zc.shape, 0)
    prev_row = zp_ref[0][SUBLANES - 1:SUBLANES, :] * jnp.where(c_idx > 0, 1.0, 0.0)
    next_row = zn_ref[0][0:1, :] * jnp.where(c_idx < n_chunks - 1, 1.0, 0.0)
    z_prev = jnp.where(row == 0, prev_row, pltpu.roll(zc, 1, 0))
    z_next = jnp.where(row == L - 1, next_row, pltpu.roll(zc, L - 1, 0))
    z = zc + mup_ref[...] * (z_prev - zc) + mun_ref[...] * (z_next - zc)

    r = z[:, 0:C]
    k = z[:, C:2 * C]
    v = z[:, 2 * C:3 * C]
    o_wd = 3 * C
    tw = jnp.tanh(z[:, o_wd:o_wd + 2 * W_RANK]).astype(BF16)
    ad = z[:, o_wd + 2 * W_RANK:o_wd + 2 * W_RANK + 2 * A_RANK].astype(BF16)
    gd = _sigmoid(z[:, o_wd + 2 * W_RANK + 2 * A_RANK:RW_COLS]).astype(BF16)
    g_ref[0] = jnp.dot(gd, gup_ref[...], preferred_element_type=F32).astype(BF16)

    ones_bd = _seg_ones()
    kappa = k * kk_ref[...]
    kh = kappa / jnp.maximum(jnp.sqrt(_segsum64(kappa * kappa, ones_bd)), 1e-12)

    ti = lax.broadcasted_iota(jnp.int32, (L, L), 0)
    si = lax.broadcasted_iota(jnp.int32, (L, L), 1)
    tp = lax.broadcasted_iota(jnp.int32, (L, PAIR), 0)
    lp = lax.broadcasted_iota(jnp.int32, (L, PAIR), 1)
    sp = lp % RW_HEAD
    lane_lo = lp < RW_HEAD
    eye_pair = jnp.where(sp == tp, 1.0, 0.0)

    kt_sum = jnp.zeros((L, C), F32)
    chains = []
    for d in range(2):
        if d == 0:
            tri = jnp.where(si <= ti, 1.0, 0.0).astype(BF16)
            m_strict, m_incl = sp < tp, sp <= tp
        else:
            tri = jnp.where(si >= ti, 1.0, 0.0).astype(BF16)
            m_strict, m_incl = sp > tp, sp >= tp
        wlog = w0_ref[d:d + 1, :] + jnp.dot(tw, wup_ref[d], preferred_element_type=F32)
        lw = -DECAY_SCALE * _sigmoid(wlog)
        a = _sigmoid(a0_ref[d:d + 1, :] + jnp.dot(ad, aup_ref[d], preferred_element_type=F32))
        kt = k * (1.0 + (a - 1.0) * ka_ref[...])
        kt_sum = kt_sum + kt
        ak = a * kh
        cum = _dot_split_rhs(tri, lw)
        ctot = cum[L - 1:L, :] if d == 0 else cum[0:1, :]
        excl = cum - lw
        a_t = -kh * jnp.exp(excl)
        q_t = r * jnp.exp(cum)
        inv = jnp.exp(-cum)
        b_t = ak * inv
        k_t = kt * inv
        suf = jnp.exp(ctot - cum)
        b_g = ak * suf
        k_g = kt * suf
        g_l = jnp.exp(ctot)

        for p in range(N_PAIR):
            sl = slice(p * PAIR, (p + 1) * PAIR)
            chains.append(dict(d=d, p=p, at=a_t[:, sl], qt=q_t[:, sl], bt=b_t[:, sl], kt=k_t[:, sl],
                               bg=b_g[:, sl], kg=k_g[:, sl], v=v[:, sl], gl=g_l[:, sl],
                               ms=m_strict, mi=m_incl))

    bonus_ref[0] = (_segsum64(r * kt_sum * rk_ref[...], ones_bd) * v).astype(BF16)

    bd = lambda x: _block_diag(x, lane_lo)
    bd2 = lambda x, y: jnp.concatenate([bd(x), bd(y)], axis=1)
    for ch in chains:
        aq = jnp.concatenate([ch["at"], ch["qt"]], axis=0)
        gram = _dot_nt(aq, jnp.concatenate([bd(ch["bt"]), bd(ch["kt"])], axis=0))
        ch["m_ab"] = jnp.where(ch["ms"], gram[:L, :PAIR], 0.0)
        ch["m_ak"] = jnp.where(ch["ms"], gram[:L, PAIR:], 0.0)
        ch["m_qb"] = jnp.where(ch["mi"], gram[L:, :PAIR], 0.0)
        ch["m_qk"] = jnp.where(ch["mi"], gram[L:, PAIR:], 0.0)
    for ch in chains:
        ch["bg_t"] = _pair_transpose(ch["bg"], lane_lo)
        ch["kg_t"] = _pair_transpose(ch["kg"], lane_lo)
    for ch in chains:
        ch["t"] = eye_pair + ch["m_ab"]
        ch["pw"] = _dot(ch["m_ab"], bd(ch["m_ab"]))
    for ch in chains:
        r3 = _dot(jnp.concatenate([ch["m_ak"], ch["m_qk"], ch["kg_t"]], axis=0), bd(ch["v"]))
        ch["p1"], ch["o1"], ch["psi"] = r3[:L], r3[L:2 * L], r3[2 * L:]
    for _ in range(4):
        for ch in chains:
            res = _dot(ch["pw"], bd2(ch["pw"], ch["t"]))
            ch["pw"] = res[:, :PAIR]
            ch["t"] = ch["t"] + res[:, PAIR:]
    for ch in chains:
        ch["t"] = ch["t"] + _dot(ch["pw"], bd(ch["t"]))
    for ch in chains:
        res = _dot(ch["t"], bd2(ch["at"], ch["p1"]))
        ch["a_p"], ch["u0"] = res[:, :PAIR], res[:, PAIR:]
    for ch in chains:
        res = _dot(jnp.concatenate([ch["m_qb"], ch["bg_t"]], axis=0), bd2(ch["a_p"], ch["u0"]))
        d, p = ch["d"], ch["p"]
        qp_ref[0, 0, d, p] = (ch["qt"] + res[:L, :PAIR]).astype(BF16)
        o1_ref[0, 0, d, p] = (ch["o1"] + res[:L, PAIR:]).astype(BF16)
        phit_ref[0, 0, d, p] = (eye_pair * ch["gl"] + res[L:, :PAIR]).astype(BF16)
        psit_ref[0, 0, d, p] = (ch["psi"] + res[L:, PAIR:]).astype(BF16)


def _rwkv_prep(z_rw, mu_prev, mu_next, w0, w_up_pad, a0, a_up_pad, g_up, k_k, k_a, r_k):
    B, T, _ = z_rw.shape
    nc = T // CHUNK
    nb8 = T // SUBLANES
    blk8 = CHUNK // SUBLANES
    vec = lambda n: _resident((1, n))
    op_spec = pl.BlockSpec((1, 1, 2, N_PAIR, CHUNK, PAIR), lambda b, c: (b, c, 0, 0, 0, 0))
    tok_spec = pl.BlockSpec((1, CHUNK, RW_WIDTH), lambda b, c: (b, c, 0))
    op_shape = (B, nc, 2, N_PAIR, CHUNK, PAIR)
    return pl.pallas_call(
        _rwkv_prep_body,
        grid=(B, nc),
        in_specs=[pl.BlockSpec((1, CHUNK, RW_COLS), lambda b, c: (b, c, 0)),
                  pl.BlockSpec((1, SUBLANES, RW_COLS), lambda b, c: (b, jnp.maximum(c * blk8 - 1, 0), 0)),
                  pl.BlockSpec((1, SUBLANES, RW_COLS), lambda b, c: (b, jnp.minimum((c + 1) * blk8, nb8 - 1), 0)),
                  vec(RW_COLS), vec(RW_COLS),
                  _resident((2, RW_WIDTH)), _resident((2, 2 * W_RANK, RW_WIDTH)),
                  _resident((2, RW_WIDTH)), _resident((2, 2 * A_RANK, RW_WIDTH)),
                  _resident((G_RANK, RW_WIDTH)),
                  vec(RW_WIDTH), vec(RW_WIDTH), vec(RW_WIDTH)],
        out_specs=[op_spec, op_spec, op_spec, op_spec, tok_spec, tok_spec],
        out_shape=[jax.ShapeDtypeStruct(op_shape, BF16), jax.ShapeDtypeStruct(op_shape, BF16),
                   jax.ShapeDtypeStruct(op_shape, BF16), jax.ShapeDtypeStruct(op_shape, BF16),
                   jax.ShapeDtypeStruct((B, T, RW_WIDTH), BF16), jax.ShapeDtypeStruct((B, T, RW_WIDTH), BF16)],
        compiler_params=_params(("parallel", "parallel")),
    )(z_rw, z_rw, z_rw, mu_prev, mu_next, w0, w_up_pad, a0, a_up_pad, g_up, k_k, k_a, r_k)


def _rwkv_scan_body(phif_ref, qpf_ref, psif_ref, o1f_ref, phib_ref, qpb_ref, psib_ref, o1b_ref,
                    of_ref, ob_ref, st_ref):
    @pl.when(pl.program_id(1) == 0)
    def _():
        st_ref[...] = jnp.zeros_like(st_ref)

    lane_lo = lax.broadcasted_iota(jnp.int32, (CHUNK, PAIR), 1) < RW_HEAD
    n_sub = phif_ref.shape[1]
    dirs = ((phif_ref, qpf_ref, psif_ref, o1f_ref, of_ref), (phib_ref, qpb_ref, psib_ref, o1b_ref, ob_ref))
    states = [[st_ref[d, p] for p in range(N_PAIR)] for d in range(2)]
    for step in range(n_sub):
        for d, (phi_ref, qp_ref, psi_ref, o1_ref, out_ref) in enumerate(dirs):
            c = step if d == 0 else n_sub - 1 - step
            for p in range(N_PAIR):
                st = states[d][p]
                st_hi = st.astype(BF16)
                st_lo = st - st_hi.astype(F32)
                lhs = jnp.concatenate([phi_ref[0, c, 0, p], qp_ref[0, c, 0, p]], axis=0)
                res = (jnp.dot(lhs, _block_diag(st_hi, lane_lo), preferred_element_type=F32)
                       + jnp.dot(lhs, _block_diag(st_lo, lane_lo), preferred_element_type=F32))
                states[d][p] = res[:CHUNK] + psi_ref[0, c, 0, p].astype(F32)
                out_ref[0, c * CHUNK:(c + 1) * CHUNK, p * PAIR:(p + 1) * PAIR] = (
                    res[CHUNK:] + o1_ref[0, c, 0, p].astype(F32)).astype(out_ref.dtype)
    for d in range(2):
        for p in range(N_PAIR):
            st_ref[d, p] = states[d][p]


def _rwkv_scan(phit, qp, psit, o1, n_sub):
    B, nc = phit.shape[0], phit.shape[1]
    T = nc * CHUNK
    nblk = nc // n_sub
    blk = (1, n_sub, 1, N_PAIR, CHUNK, PAIR)
    fwd = pl.BlockSpec(blk, lambda b, j: (b, j, 0, 0, 0, 0))
    bwd = pl.BlockSpec(blk, lambda b, j: (b, nblk - 1 - j, 1, 0, 0, 0))
    return pl.pallas_call(
        _rwkv_scan_body,
        grid=(B, nblk),
        in_specs=[fwd, fwd, fwd, fwd, bwd, bwd, bwd, bwd],
        out_specs=[pl.BlockSpec((1, n_sub * CHUNK, RW_WIDTH), lambda b, j: (b, j, 0)),
                   pl.BlockSpec((1, n_sub * CHUNK, RW_WIDTH), lambda b, j: (b, nblk - 1 - j, 0))],
        out_shape=[jax.ShapeDtypeStruct((B, T, RW_WIDTH), BF16), jax.ShapeDtypeStruct((B, T, RW_WIDTH), BF16)],
        scratch_shapes=[pltpu.VMEM((2, N_PAIR, CHUNK, PAIR), F32)],
        compiler_params=_params(("parallel", "arbitrary")),
    )(phit, qp, psit, o1, phit, qp, psit, o1)


def _attn_body(q_ref, k_ref, v_ref, o_ref, m_ref, l_ref, acc_ref):
    ki = pl.program_id(3)

    @pl.when(ki == 0)
    def _():
        m_ref[...] = jnp.full_like(m_ref, -jnp.inf)
        l_ref[...] = jnp.zeros_like(l_ref)
        acc_ref[...] = jnp.zeros_like(acc_ref)

    q = q_ref[0]
    tq = q.shape[0]
    k = k_ref[0]
    v = v_ref[0]
    heads_per_pass = ATT_GROUP // N_ATT_PASS
    for hp in range(N_ATT_PASS):
        rows = slice(hp * heads_per_pass * tq, (hp + 1) * heads_per_pass * tq)
        qh = jnp.concatenate([q[:, g * ATT_HEAD:(g + 1) * ATT_HEAD]
                              for g in range(hp * heads_per_pass, (hp + 1) * heads_per_pass)], axis=0)
        s = lax.dot_general(qh, k, (((1,), (1,)), ((), ())), preferred_element_type=F32)
        m_prev = m_ref[rows, :]
        m_cur = jnp.maximum(m_prev, jnp.max(s, axis=-1, keepdims=True))
        alpha = jnp.exp2(m_prev - m_cur)
        p = jnp.exp2(s - m_cur[:, 0:1])
        l_ref[rows, :] = alpha * l_ref[rows, :] + jnp.sum(p, axis=-1, keepdims=True)
        acc_ref[rows, :] = alpha * acc_ref[rows, :] + jnp.dot(p.astype(BF16), v, preferred_element_type=F32)
        m_ref[rows, :] = m_cur

    @pl.when(ki == pl.num_programs(3) - 1)
    def _():
        o = acc_ref[...] / l_ref[...]
        for g in range(ATT_GROUP):
            o_ref[0, :, g * ATT_HEAD:(g + 1) * ATT_HEAD] = o[g * tq:(g + 1) * tq].astype(o_ref.dtype)


def _attention(q, k, v, tq, tk):
    B, T, _ = q.shape
    gw = ATT_GROUP * ATT_HEAD
    return pl.pallas_call(
        _attn_body,
        grid=(B, ATT_KV_HEADS, T // tq, T // tk),
        in_specs=[pl.BlockSpec((1, tq, gw), lambda b, h, i, j: (b, i, h)),
                  pl.BlockSpec((1, tk, ATT_HEAD), lambda b, h, i, j: (b, j, h)),
                  pl.BlockSpec((1, tk, ATT_HEAD), lambda b, h, i, j: (b, j, h))],
        out_specs=pl.BlockSpec((1, tq, gw), lambda b, h, i, j: (b, i, h)),
        out_shape=jax.ShapeDtypeStruct((B, T, ATT_Q), BF16),
        scratch_shapes=[pltpu.VMEM((ATT_GROUP * tq, ATT_HEAD), F32),
                        pltpu.VMEM((ATT_GROUP * tq, ATT_HEAD), F32),
                        pltpu.VMEM((ATT_GROUP * tq, ATT_HEAD), F32)],
        compiler_params=_params(("parallel", "parallel", "parallel", "arbitrary")),
    )(q, k, v)


def _layernorm(y, g, b, eps):
    mu = jnp.mean(y, axis=-1, keepdims=True)
    d = y - mu
    var = jnp.mean(d * d, axis=-1, keepdims=True)
    return d * lax.rsqrt(var + eps) * g + b


def _merge_body(x_ref, of_ref, ob_ref, bonus_ref, g_ref, oatt_ref, gate_ref, gng_ref, gnb_ref,
                wprw_ref, wpatt_ref, wout_ref, ln1g_ref, ln1b_ref, h_ref):
    ones_bd = _seg_ones()
    o = of_ref[...].astype(F32) + ob_ref[...].astype(F32)
    inv_n = 1.0 / RW_HEAD
    mu = _segsum64(o, ones_bd) * inv_n
    d = o - mu
    var = _segsum64(d * d, ones_bd) * inv_n
    on = d * lax.rsqrt(var + GN_EPS) * gng_ref[...] + gnb_ref[...]
    o_rw = ((on + bonus_ref[...].astype(F32)) * g_ref[...].astype(F32)).astype(BF16)
    gates = gate_ref[...].astype(F32)
    merged = (gates[:, :D_MODEL] * jnp.dot(o_rw, wprw_ref[...], preferred_element_type=F32)
              + gates[:, D_MODEL:] * jnp.dot(oatt_ref[...], wpatt_ref[...], preferred_element_type=F32))
    mix = jnp.dot(merged.astype(BF16), wout_ref[...], preferred_element_type=F32)
    h_ref[...] = _layernorm(ALPHA * x_ref[...] + mix, ln1g_ref[...], ln1b_ref[...], LN_EPS)


def _merge(x, o_f, o_b, bonus, g, o_att, gates, gn_g, gn_b, w_prw, w_patt, w_out, ln1_g, ln1_b, tm):
    n = x.shape[0]
    tok = lambda w: pl.BlockSpec((tm, w), lambda i: (i, 0))
    vec = _resident((1, D_MODEL))
    mat = _resident((D_MODEL, D_MODEL))
    return pl.pallas_call(
        _merge_body,
        grid=(n // tm,),
        in_specs=[tok(D_MODEL), tok(D_MODEL), tok(D_MODEL), tok(D_MODEL), tok(D_MODEL), tok(D_MODEL),
                  tok(GATE_COLS), vec, vec, mat, mat, mat, vec, vec],
        out_specs=tok(D_MODEL),
        out_shape=jax.ShapeDtypeStruct((n, D_MODEL), F32),
        compiler_params=_params(("parallel",)),
    )(x, o_f, o_b, bonus, g, o_att, gates, gn_g, gn_b, w_prw, w_patt, w_out, ln1_g, ln1_b)


def _mlp_body(h_ref, w1_ref, w2_ref, g_ref, b_ref, o_ref):
    h = h_ref[...]
    u = jnp.maximum(jnp.dot(h.astype(BF16), w1_ref[...], preferred_element_type=F32), 0.0)
    ff = jnp.dot((u * u).astype(BF16), w2_ref[...], preferred_element_type=F32)
    o_ref[...] = _layernorm(ALPHA * h + ff, g_ref[...], b_ref[...], LN_EPS)


def _mlp(h, w1, w2, ln_g, ln_b, tm):
    n = h.shape[0]
    return pl.pallas_call(
        _mlp_body,
        grid=(n // tm,),
        in_specs=[pl.BlockSpec((tm, D_MODEL), lambda i: (i, 0)),
                  _resident((D_MODEL, D_FF)), _resident((D_FF, D_MODEL)),
                  _resident((1, D_MODEL)), _resident((1, D_MODEL))],
        out_specs=pl.BlockSpec((tm, D_MODEL), lambda i: (i, 0)),
        out_shape=jax.ShapeDtypeStruct((n, D_MODEL), F32),
        compiler_params=_params(("parallel",)),
    )(h, w1, w2, ln_g, ln_b)


def _axial_rope_tables(T):
    rows = T // GRID_W
    row = jnp.repeat(jnp.arange(rows, dtype=F32), GRID_W)
    col = jnp.tile(jnp.arange(GRID_W, dtype=F32), rows)
    half = ATT_HEAD // 2
    inv = ROPE_THETA ** (-jnp.arange(0, half, 2, dtype=F32) / half)
    ang = jnp.stack([row[:, None] * inv, col[:, None] * inv], axis=1)
    ang = jnp.broadcast_to(ang[:, :, None, :], (T, 2, 2, half // 2)).reshape(T, ATT_HEAD)
    return jnp.cos(ang), jnp.sin(ang)


def _pad_rank(w):
    z = jnp.zeros_like(w[0])
    return jnp.stack([jnp.concatenate([w[0], z], axis=0), jnp.concatenate([z, w[1]], axis=0)]).astype(BF16)


def _tile(n, pref):
    t = min(pref, n)
    while n % t:
        t //= 2
    return t


def _layer(x, p):
    B, T, D = x.shape
    n = B * T
    xf = x.reshape(n, D)
    cos, sin = _axial_rope_tables(T)
    z_rw, q_r, k_r, v_b, gates = _inproj(xf, p["w_rw"], p["w_qkv"], p["w_gate"], cos, sin,
                                         p["q_norm"], p["k_norm"], _tile(T, 256), T)

    phit, qp, psit, o1, bonus, g = _rwkv_prep(
        z_rw.reshape(B, T, RW_COLS), p["mu_prev"], p["mu_next"], p["w0"], p["w_up"], p["a0"], p["a_up"],
        p["g_up"], p["k_k"], p["k_a"], p["r_k"])
    o_f, o_b = _rwkv_scan(phit, qp, psit, o1, _tile(T // CHUNK, 2))

    o_att = _attention(q_r.reshape(B, T, ATT_Q), k_r.reshape(B, T, ATT_KV), v_b.reshape(B, T, ATT_KV),
                       _tile(T, 256), _tile(T, 2048))

    h = _merge(xf, o_f.reshape(n, D), o_b.reshape(n, D), bonus.reshape(n, D), g.reshape(n, D),
               o_att.reshape(n, D), gates, p["gn_g"], p["gn_b"], p["w_prw"], p["w_patt"], p["w_out"],
               p["ln1_g"], p["ln1_b"], _tile(n, 256))
    y = _mlp(h, p["w_ff1"], p["w_ff2"], p["ln2_g"], p["ln2_b"], _tile(n, 512))
    return y.reshape(B, T, D)


def kernel(x_prompt, x_sample, w_in, rw_mu_prev, rw_mu_next, rw_w0, rw_w_up, rw_a0, rw_a_up, rw_g_up, rw_k_k,
           rw_k_a, rw_r_k, rw_gn_g, rw_gn_b, q_norm, k_norm, w_proj_rwkv, w_proj_attn, w_out, ln1_g, ln1_b,
           w_ff1, w_ff2, ln2_g, ln2_b):
    def layer_params(l):
        w = w_in[l].astype(BF16)
        row = lambda a: a[l].reshape(1, -1)
        return dict(
            w_rw=w[:, :RW_COLS], w_qkv=w[:, RW_COLS:RW_COLS + QKV_COLS], w_gate=w[:, RW_COLS + QKV_COLS:],
            mu_prev=row(rw_mu_prev), mu_next=row(rw_mu_next),
            w0=rw_w0[l], w_up=_pad_rank(rw_w_up[l]), a0=rw_a0[l], a_up=_pad_rank(rw_a_up[l]),
            g_up=rw_g_up[l].astype(BF16), k_k=row(rw_k_k), k_a=row(rw_k_a), r_k=row(rw_r_k),
            gn_g=row(rw_gn_g), gn_b=row(rw_gn_b), q_norm=row(q_norm), k_norm=row(k_norm),
            w_prw=w_proj_rwkv[l].astype(BF16), w_patt=w_proj_attn[l].astype(BF16), w_out=w_out[l].astype(BF16),
            ln1_g=row(ln1_g), ln1_b=row(ln1_b), w_ff1=w_ff1[l].astype(BF16), w_ff2=w_ff2[l].astype(BF16),
            ln2_g=row(ln2_g), ln2_b=row(ln2_b))

    layers = [layer_params(l) for l in range(w_in.shape[0])]

    def trunk(x):
        for p in layers:
            x = _layer(x, p)
        return x

    return trunk(x_prompt), trunk(x_sample)
```

```python
import functools
import math

import jax
import jax.numpy as jnp
from jax import lax
from jax.experimental import pallas as pl
from jax.experimental.pallas import tpu as pltpu

F32 = jnp.float32
BF16 = jnp.bfloat16

D_MODEL = 1024
GRID_W = 64
RW_HEAD = 64
RW_WIDTH = 1024
W_RANK = 64
A_RANK = 64
G_RANK = 128
DECAY_SCALE = math.exp(-0.5)
GN_EPS = 64e-5
ATT_HEAD = 128
ATT_Q_HEADS = 8
ATT_KV_HEADS = 2
ATT_GROUP = ATT_Q_HEADS // ATT_KV_HEADS
ATT_Q = ATT_Q_HEADS * ATT_HEAD
ATT_KV = ATT_KV_HEADS * ATT_HEAD
ROPE_THETA = 10000.0
RMS_EPS = 1e-6
Q_SCALE = ATT_HEAD ** -0.5 * math.log2(math.e)
D_FF = 4 * D_MODEL
LN_EPS = 1e-5
DEPTH = 1
ALPHA = (2 * DEPTH) ** 0.25
RW_COLS = 3 * RW_WIDTH + 2 * W_RANK + 2 * A_RANK + G_RANK
QKV_COLS = ATT_Q + 2 * ATT_KV
GATE_COLS = 2 * D_MODEL

SCORE_LIMIT = 96.0
N_ATT_PASS = 1
CHUNK = 64
PAIR = 2 * RW_HEAD
N_PAIR = RW_WIDTH // PAIR
LANES = 128
SUBLANES = 8
VMEM_LIMIT = 56 * 1024 * 1024


def _params(sem):
    return pltpu.CompilerParams(dimension_semantics=sem, vmem_limit_bytes=VMEM_LIMIT)


def _dot(a, b):
    return jnp.dot(a.astype(BF16), b.astype(BF16), preferred_element_type=F32)


def _dot_nt(a, b):
    return lax.dot_general(a.astype(BF16), b.astype(BF16), (((1,), (1,)), ((), ())),
                           preferred_element_type=F32)


def _split(x):
    hi = x.astype(BF16)
    lo = (x - hi.astype(F32)).astype(BF16)
    return hi, lo


def _dot_split_lhs(x, w):
    hi, lo = _split(x)
    return (jnp.dot(hi, w, preferred_element_type=F32) + jnp.dot(lo, w, preferred_element_type=F32))


def _dot_split_rhs(w, x):
    hi, lo = _split(x)
    return (jnp.dot(w, hi, preferred_element_type=F32) + jnp.dot(w, lo, preferred_element_type=F32))


def _sigmoid(x):
    return 0.5 * jnp.tanh(0.5 * x) + 0.5


def _resident(shape):
    nd = len(shape)
    return pl.BlockSpec(shape, lambda *_: (0,) * nd, pipeline_mode=pl.Buffered(1))


def _seg_ones():
    r = lax.broadcasted_iota(jnp.int32, (LANES, LANES), 0) // RW_HEAD
    c = lax.broadcasted_iota(jnp.int32, (LANES, LANES), 1) // RW_HEAD
    return jnp.where(r == c, 1.0, 0.0).astype(BF16)


def _segsum64(x, ones_bd):
    parts = [_dot_split_lhs(x[:, j * LANES:(j + 1) * LANES], ones_bd) for j in range(x.shape[1] // LANES)]
    return jnp.concatenate(parts, axis=1)


def _norm_rope(xh, gain, cos, sin, first):
    ms = jnp.mean(xh * xh, axis=-1, keepdims=True)
    xn = xh * lax.rsqrt(ms + RMS_EPS) * gain
    rot = jnp.where(first, -pltpu.roll(xn, ATT_HEAD - ATT_HEAD // 4, 1), pltpu.roll(xn, ATT_HEAD // 4, 1))
    return xn * cos + rot * sin


def _inproj_body(x_ref, wrw_ref, wqkv_ref, wg_ref, cos_ref, sin_ref, qn_ref, kn_ref,
                 zrw_ref, q_ref, k_ref, v_ref, gate_ref):
    x = x_ref[...].astype(BF16)
    zrw_ref[...] = jnp.dot(x, wrw_ref[...], preferred_element_type=F32)
    gate_ref[...] = _sigmoid(jnp.dot(x, wg_ref[...], preferred_element_type=F32)).astype(BF16)
    qkv = jnp.dot(x, wqkv_ref[...], preferred_element_type=F32)
    cos = cos_ref[...]
    sin = sin_ref[...]
    lane = lax.broadcasted_iota(jnp.int32, cos.shape, 1)
    first = (lane % (ATT_HEAD // 2)) < (ATT_HEAD // 4)
    for h in range(ATT_Q_HEADS):
        sl = slice(h * ATT_HEAD, (h + 1) * ATT_HEAD)
        q_ref[:, sl] = (_norm_rope(qkv[:, sl], qn_ref[...], cos, sin, first) * Q_SCALE).astype(BF16)
    for h in range(ATT_KV_HEADS):
        sl = slice(h * ATT_HEAD, (h + 1) * ATT_HEAD)
        k_ref[:, sl] = _norm_rope(qkv[:, ATT_Q + h * ATT_HEAD:ATT_Q + (h + 1) * ATT_HEAD], kn_ref[...],
                                  cos, sin, first).astype(BF16)
    v_ref[...] = qkv[:, ATT_Q + ATT_KV:].astype(BF16)


def _inproj(x, w_rw, w_qkv, w_gate, cos, sin, q_norm, k_norm, tm, seq_len):
    n = x.shape[0]
    tiles_per_seq = seq_len // tm
    tok = lambda w: pl.BlockSpec((tm, w), lambda i: (i, 0))
    rope = pl.BlockSpec((tm, ATT_HEAD), lambda i: (i % tiles_per_seq, 0))
    return pl.pallas_call(
        _inproj_body,
        grid=(n // tm,),
        in_specs=[tok(D_MODEL),
                  _resident((D_MODEL, RW_COLS)), _resident((D_MODEL, QKV_COLS)), _resident((D_MODEL, GATE_COLS)),
                  rope, rope, _resident((1, ATT_HEAD)), _resident((1, ATT_HEAD))],
        out_specs=[tok(RW_COLS), tok(ATT_Q), tok(ATT_KV), tok(ATT_KV), tok(GATE_COLS)],
        out_shape=[jax.ShapeDtypeStruct((n, RW_COLS), F32),
                   jax.ShapeDtypeStruct((n, ATT_Q), BF16),
                   jax.ShapeDtypeStruct((n, ATT_KV), BF16),
                   jax.ShapeDtypeStruct((n, ATT_KV), BF16),
                   jax.ShapeDtypeStruct((n, GATE_COLS), BF16)],
        compiler_params=_params(("parallel",)),
    )(x, w_rw, w_qkv, w_gate, cos, sin, q_norm, k_norm)


def _block_diag(x, lane_lo):
    top = jnp.where(lane_lo, x, 0.0)
    bot = jnp.where(lane_lo, 0.0, x)
    return jnp.concatenate([top, bot], axis=0).astype(BF16)


def _pair_transpose(x, lane_lo):
    top = jnp.where(lane_lo, x, 0.0)
    bot = jnp.where(lane_lo, 0.0, x)
    xt = jnp.transpose(jnp.concatenate([top, bot], axis=0))
    return xt[:CHUNK] + xt[CHUNK:]


def _rwkv_prep_body(zc_ref, zp_ref, zn_ref, mup_ref, mun_ref, w0_ref, wup_ref, a0_ref, aup_ref, gup_ref,
                    kk_ref, ka_ref, rk_ref,
                    phit_ref, qp_ref, psit_ref, o1_ref, bonus_ref, g_ref):
    c_idx = pl.program_id(1)
    n_chunks = pl.num_programs(1)
    L = CHUNK
    C = RW_WIDTH

    zc = zc_ref[0]
    row = lax.broadcasted_iota(jnp.int32, zc.shape, 0)
    prev_row = zp_ref[0][SUBLANES - 1:SUBLANES, :] * jnp.where(c_idx > 0, 1.0, 0.0)
    next_row = zn_ref[0][0:1, :] * jnp.where(c_idx < n_chunks - 1, 1.0, 0.0)
    z_prev = jnp.where(row == 0, prev_row, pltpu.roll(zc, 1, 0))
    z_next = jnp.where(row == L - 1, next_row, pltpu.roll(zc, L - 1, 0))
    z = zc + mup_ref[...] * (z_prev - zc) + mun_ref[...] * (z_next - zc)

    r = z[:, 0:C]
    k = z[:, C:2 * C]
    v = z[:, 2 * C:3 * C]
    o_wd = 3 * C
    tw = jnp.tanh(z[:, o_wd:o_wd + 2 * W_RANK]).astype(BF16)
    ad = z[:, o_wd + 2 * W_RANK:o_wd + 2 * W_RANK + 2 * A_RANK].astype(BF16)
    gd = _sigmoid(z[:, o_wd + 2 * W_RANK + 2 * A_RANK:RW_COLS]).astype(BF16)
    g_ref[0] = jnp.dot(gd, gup_ref[...], preferred_element_type=F32).astype(BF16)

    ones_bd = _seg_ones()
    kappa = k * kk_ref[...]
    kh = kappa / jnp.maximum(jnp.sqrt(_segsum64(kappa * kappa, ones_bd)), 1e-12)

    ti = lax.broadcasted_iota(jnp.int32, (L, L), 0)
    si = lax.broadcasted_iota(jnp.int32, (L, L), 1)
    tp = lax.broadcasted_iota(jnp.int32, (L, PAIR), 0)
    lp = lax.broadcasted_iota(jnp.int32, (L, PAIR), 1)
    sp = lp % RW_HEAD
    lane_lo = lp < RW_HEAD
    eye_pair = jnp.where(sp == tp, 1.0, 0.0)

    kt_sum = jnp.zeros((L, C), F32)
    chains = []
    for d in range(2):
        if d == 0:
            tri = jnp.where(si <= ti, 1.0, 0.0).astype(BF16)
            m_strict, m_incl = sp < tp, sp <= tp
        else:
            tri = jnp.where(si >= ti, 1.0, 0.0).astype(BF16)
            m_strict, m_incl = sp > tp, sp >= tp
        wlog = w0_ref[d:d + 1, :] + jnp.dot(tw, wup_ref[d], preferred_element_type=F32)
        lw = -DECAY_SCALE * _sigmoid(wlog)
        a = _sigmoid(a0_ref[d:d + 1, :] + jnp.dot(ad, aup_ref[d], preferred_element_type=F32))
        kt = k * (1.0 + (a - 1.0) * ka_ref[...])
        kt_sum = kt_sum + kt
        ak = a * kh
        cum = _dot_split_rhs(tri, lw)
        ctot = cum[L - 1:L, :] if d == 0 else cum[0:1, :]
        excl = cum - lw
        a_t = -kh * jnp.exp(excl)
        q_t = r * jnp.exp(cum)
        inv = jnp.exp(-cum)
        b_t = ak * inv
        k_t = kt * inv
        suf = jnp.exp(ctot - cum)
        b_g = ak * suf
        k_g = kt * suf
        g_l = jnp.exp(ctot)

        for p in range(N_PAIR):
            sl = slice(p * PAIR, (p + 1) * PAIR)
            chains.append(dict(d=d, p=p, at=a_t[:, sl], qt=q_t[:, sl], bt=b_t[:, sl], kt=k_t[:, sl],
                               bg=b_g[:, sl], kg=k_g[:, sl], v=v[:, sl], gl=g_l[:, sl],
                               ms=m_strict, mi=m_incl))

    bonus_ref[0] = (_segsum64(r * kt_sum * rk_ref[...], ones_bd) * v).astype(BF16)

    bd = lambda x: _block_diag(x, lane_lo)
    bd2 = lambda x, y: jnp.concatenate([bd(x), bd(y)], axis=1)
    for ch in chains:
        aq = jnp.concatenate([ch["at"], ch["qt"]], axis=0)
        gram = _dot_nt(aq, jnp.concatenate([bd(ch["bt"]), bd(ch["kt"])], axis=0))
        ch["m_ab"] = jnp.where(ch["ms"], gram[:L, :PAIR], 0.0)
        ch["m_ak"] = jnp.where(ch["ms"], gram[:L, PAIR:], 0.0)
        ch["m_qb"] = jnp.where(ch["mi"], gram[L:, :PAIR], 0.0)
        ch["m_qk"] = jnp.where(ch["mi"], gram[L:, PAIR:], 0.0)
    for ch in chains:
        ch["bg_t"] = _pair_transpose(ch["bg"], lane_lo)
        ch["kg_t"] = _pair_transpose(ch["kg"], lane_lo)
    for ch in chains:
        ch["t"] = eye_pair + ch["m_ab"]
        ch["pw"] = _dot(ch["m_ab"], bd(ch["m_ab"]))
    for ch in chains:
        r3 = _dot(jnp.concatenate([ch["m_ak"], ch["m_qk"], ch["kg_t"]], axis=0), bd(ch["v"]))
        ch["p1"], ch["o1"], ch["psi"] = r3[:L], r3[L:2 * L], r3[2 * L:]
    for _ in range(4):
        for ch in chains:
            res = _dot(ch["pw"], bd2(ch["pw"], ch["t"]))
            ch["pw"] = res[:, :PAIR]
            ch["t"] = ch["t"] + res[:, PAIR:]
    for ch in chains:
        ch["t"] = ch["t"] + _dot(ch["pw"], bd(ch["t"]))
    for ch in chains:
        res = _dot(ch["t"], bd2(ch["at"], ch["p1"]))
        ch["a_p"], ch["u0"] = res[:, :PAIR], res[:, PAIR:]
    for ch in chains:
        res = _dot(jnp.concatenate([ch["m_qb"], ch["bg_t"]], axis=0), bd2(ch["a_p"], ch["u0"]))
        d, p = ch["d"], ch["p"]
        qp_ref[0, 0, d, p] = (ch["qt"] + res[:L, :PAIR]).astype(BF16)
        o1_ref[0, 0, d, p] = (ch["o1"] + res[:L, PAIR:]).astype(BF16)
        phit_ref[0, 0, d, p] = (eye_pair * ch["gl"] + res[L:, :PAIR]).astype(BF16)
        psit_ref[0, 0, d, p] = (ch["psi"] + res[L:, PAIR:]).astype(BF16)


def _rwkv_prep(z_rw, mu_prev, mu_next, w0, w_up_pad, a0, a_up_pad, g_up, k_k, k_a, r_k):
    B, T, _ = z_rw.shape
    nc = T // CHUNK
    nb8 = T // SUBLANES
    blk8 = CHUNK // SUBLANES
    vec = lambda n: _resident((1, n))
    op_spec = pl.BlockSpec((1, 1, 2, N_PAIR, CHUNK, PAIR), lambda b, c: (b, c, 0, 0, 0, 0))
    tok_spec = pl.BlockSpec((1, CHUNK, RW_WIDTH), lambda b, c: (b, c, 0))
    op_shape = (B, nc, 2, N_PAIR, CHUNK, PAIR)
    return pl.pallas_call(
        _rwkv_prep_body,
        grid=(B, nc),
        in_specs=[pl.BlockSpec((1, CHUNK, RW_COLS), lambda b, c: (b, c, 0)),
                  pl.BlockSpec((1, SUBLANES, RW_COLS), lambda b, c: (b, jnp.maximum(c * blk8 - 1, 0), 0)),
                  pl.BlockSpec((1, SUBLANES, RW_COLS), lambda b, c: (b, jnp.minimum((c + 1) * blk8, nb8 - 1), 0)),
                  vec(RW_COLS), vec(RW_COLS),
                  _resident((2, RW_WIDTH)), _resident((2, 2 * W_RANK, RW_WIDTH)),
                  _resident((2, RW_WIDTH)), _resident((2, 2 * A_RANK, RW_WIDTH)),
                  _resident((G_RANK, RW_WIDTH)),
                  vec(RW_WIDTH), vec(RW_WIDTH), vec(RW_WIDTH)],
        out_specs=[op_spec, op_spec, op_spec, op_spec, tok_spec, tok_spec],
        out_shape=[jax.ShapeDtypeStruct(op_shape, BF16), jax.ShapeDtypeStruct(op_shape, BF16),
                   jax.ShapeDtypeStruct(op_shape, BF16), jax.ShapeDtypeStruct(op_shape, BF16),
                   jax.ShapeDtypeStruct((B, T, RW_WIDTH), BF16), jax.ShapeDtypeStruct((B, T, RW_WIDTH), BF16)],
        compiler_params=_params(("parallel", "parallel")),
    )(z_rw, z_rw, z_rw, mu_prev, mu_next, w0, w_up_pad, a0, a_up_pad, g_up, k_k, k_a, r_k)


def _rwkv_scan_body(phif_ref, qpf_ref, psif_ref, o1f_ref, phib_ref, qpb_ref, psib_ref, o1b_ref,
                    of_ref, ob_ref, st_ref):
    @pl.when(pl.program_id(1) == 0)
    def _():
        st_ref[...] = jnp.zeros_like(st_ref)

    lane_lo = lax.broadcasted_iota(jnp.int32, (CHUNK, PAIR), 1) < RW_HEAD
    n_sub = phif_ref.shape[1]
    dirs = ((phif_ref, qpf_ref, psif_ref, o1f_ref, of_ref), (phib_ref, qpb_ref, psib_ref, o1b_ref, ob_ref))
    states = [[st_ref[d, p] for p in range(N_PAIR)] for d in range(2)]
    for step in range(n_sub):
        for d, (phi_ref, qp_ref, psi_ref, o1_ref, out_ref) in enumerate(dirs):
            c = step if d == 0 else n_sub - 1 - step
            for p in range(N_PAIR):
                st = states[d][p]
                st_hi = st.astype(BF16)
                st_lo = st - st_hi.astype(F32)
                lhs = jnp.concatenate([phi_ref[0, c, 0, p], qp_ref[0, c, 0, p]], axis=0)
                res = (jnp.dot(lhs, _block_diag(st_hi, lane_lo), preferred_element_type=F32)
                       + jnp.dot(lhs, _block_diag(st_lo, lane_lo), preferred_element_type=F32))
                states[d][p] = res[:CHUNK] + psi_ref[0, c, 0, p].astype(F32)
                out_ref[0, c * CHUNK:(c + 1) * CHUNK, p * PAIR:(p + 1) * PAIR] = (
                    res[CHUNK:] + o1_ref[0, c, 0, p].astype(F32)).astype(out_ref.dtype)
    for d in range(2):
        for p in range(N_PAIR):
            st_ref[d, p] = states[d][p]


def _rwkv_scan(phit, qp, psit, o1, n_sub):
    B, nc = phit.shape[0], phit.shape[1]
    T = nc * CHUNK
    nblk = nc // n_sub
    blk = (1, n_sub, 1, N_PAIR, CHUNK, PAIR)
    fwd = pl.BlockSpec(blk, lambda b, j: (b, j, 0, 0, 0, 0))
    bwd = pl.BlockSpec(blk, lambda b, j: (b, nblk - 1 - j, 1, 0, 0, 0))
    return pl.pallas_call(
        _rwkv_scan_body,
        grid=(B, nblk),
        in_specs=[fwd, fwd, fwd, fwd, bwd, bwd, bwd, bwd],
        out_specs=[pl.BlockSpec((1, n_sub * CHUNK, RW_WIDTH), lambda b, j: (b, j, 0)),
                   pl.BlockSpec((1, n_sub * CHUNK, RW_WIDTH), lambda b, j: (b, nblk - 1 - j, 0))],
        out_shape=[jax.ShapeDtypeStruct((B, T, RW_WIDTH), BF16), jax.ShapeDtypeStruct((B, T, RW_WIDTH), BF16)],
        scratch_shapes=[pltpu.VMEM((2, N_PAIR, CHUNK, PAIR), F32)],
        compiler_params=_params(("parallel", "arbitrary")),
    )(phit, qp, psit, o1, phit, qp, psit, o1)


def _attn_body(q_ref, k_ref, v_ref, o_ref, m_ref, l_ref, acc_ref):
    ki = pl.program_id(3)

    @pl.when(ki == 0)
    def _():
        m_ref[...] = jnp.full_like(m_ref, -jnp.inf)
        l_ref[...] = jnp.zeros_like(l_ref)
        acc_ref[...] = jnp.zeros_like(acc_ref)

    q = q_ref[0]
    tq = q.shape[0]
    k = k_ref[0]
    v = v_ref[0]
    heads_per_pass = ATT_GROUP // N_ATT_PASS
    for hp in range(N_ATT_PASS):
        rows = slice(hp * heads_per_pass * tq, (hp + 1) * heads_per_pass * tq)
        qh = jnp.concatenate([q[:, g * ATT_HEAD:(g + 1) * ATT_HEAD]
                              for g in range(hp * heads_per_pass, (hp + 1) * heads_per_pass)], axis=0)
        s = lax.dot_general(qh, k, (((1,), (1,)), ((), ())), preferred_element_type=F32)
        m_prev = m_ref[rows, :]
        m_cur = jnp.maximum(m_prev, jnp.max(s, axis=-1, keepdims=True))
        alpha = jnp.exp2(m_prev - m_cur)
        p = jnp.exp2(s - m_cur[:, 0:1])
        l_ref[rows, :] = alpha * l_ref[rows, :] + jnp.sum(p, axis=-1, keepdims=True)
        acc_ref[rows, :] = alpha * acc_ref[rows, :] + jnp.dot(p.astype(BF16), v, preferred_element_type=F32)
        m_ref[rows, :] = m_cur

    @pl.when(ki == pl.num_programs(3) - 1)
    def _():
        o = acc_ref[...] / l_ref[...]
        for g in range(ATT_GROUP):
            o_ref[0, :, g * ATT_HEAD:(g + 1) * ATT_HEAD] = o[g * tq:(g + 1) * tq].astype(o_ref.dtype)


def _attn_unshifted_body(q_ref, k_ref, v_ref, o_ref, l_ref, acc_ref):
    ki = pl.program_id(3)

    @pl.when(ki == 0)
    def _():
        l_ref[...] = jnp.zeros_like(l_ref)
        acc_ref[...] = jnp.zeros_like(acc_ref)

    q = q_ref[0]
    tq = q.shape[0]
    q4 = jnp.concatenate([q[:, g * ATT_HEAD:(g + 1) * ATT_HEAD] for g in range(ATT_GROUP)], axis=0)
    s = lax.dot_general(q4, k_ref[0], (((1,), (1,)), ((), ())), preferred_element_type=F32)
    p = jnp.exp2(s)
    l_ref[...] = l_ref[...] + jnp.sum(p, axis=-1, keepdims=True)
    acc_ref[...] = acc_ref[...] + jnp.dot(p.astype(BF16), v_ref[0], preferred_element_type=F32)

    @pl.when(ki == pl.num_programs(3) - 1)
    def _():
        o = acc_ref[...] / l_ref[...]
        for g in range(ATT_GROUP):
            o_ref[0, :, g * ATT_HEAD:(g + 1) * ATT_HEAD] = o[g * tq:(g + 1) * tq].astype(o_ref.dtype)


def _attention_call(body, n_stat, q, k, v, tq, tk):
    B, T, _ = q.shape
    gw = ATT_GROUP * ATT_HEAD
    return pl.pallas_call(
        body,
        grid=(B, ATT_KV_HEADS, T // tq, T // tk),
        in_specs=[pl.BlockSpec((1, tq, gw), lambda b, h, i, j: (b, i, h)),
                  pl.BlockSpec((1, tk, ATT_HEAD), lambda b, h, i, j: (b, j, h)),
                  pl.BlockSpec((1, tk, ATT_HEAD), lambda b, h, i, j: (b, j, h))],
        out_specs=pl.BlockSpec((1, tq, gw), lambda b, h, i, j: (b, i, h)),
        out_shape=jax.ShapeDtypeStruct((B, T, ATT_Q), BF16),
        scratch_shapes=[pltpu.VMEM((ATT_GROUP * tq, ATT_HEAD), F32) for _ in range(n_stat)],
        compiler_params=_params(("parallel", "parallel", "parallel", "arbitrary")),
    )(q, k, v)


def _attention(q, k, v, score_bound, tq, tk):
    return lax.cond(score_bound <= SCORE_LIMIT,
                    lambda: _attention_call(_attn_unshifted_body, 2, q, k, v, tq, tk),
                    lambda: _attention_call(_attn_body, 3, q, k, v, tq, tk))


def _layernorm(y, g, b, eps):
    mu = jnp.mean(y, axis=-1, keepdims=True)
    d = y - mu
    var = jnp.mean(d * d, axis=-1, keepdims=True)
    return d * lax.rsqrt(var + eps) * g + b


def _merge_body(x_ref, of_ref, ob_ref, bonus_ref, g_ref, oatt_ref, gate_ref, gng_ref, gnb_ref,
                wprw_ref, wpatt_ref, wout_ref, ln1g_ref, ln1b_ref, h_ref):
    ones_bd = _seg_ones()
    o = of_ref[...].astype(F32) + ob_ref[...].astype(F32)
    inv_n = 1.0 / RW_HEAD
    mu = _segsum64(o, ones_bd) * inv_n
    d = o - mu
    var = _segsum64(d * d, ones_bd) * inv_n
    on = d * lax.rsqrt(var + GN_EPS) * gng_ref[...] + gnb_ref[...]
    o_rw = ((on + bonus_ref[...].astype(F32)) * g_ref[...].astype(F32)).astype(BF16)
    gates = gate_ref[...].astype(F32)
    merged = (gates[:, :D_MODEL] * jnp.dot(o_rw, wprw_ref[...], preferred_element_type=F32)
              + gates[:, D_MODEL:] * jnp.dot(oatt_ref[...], wpatt_ref[...], preferred_element_type=F32))
    mix = jnp.dot(merged.astype(BF16), wout_ref[...], preferred_element_type=F32)
    h_ref[...] = _layernorm(ALPHA * x_ref[...] + mix, ln1g_ref[...], ln1b_ref[...], LN_EPS)


def _merge(x, o_f, o_b, bonus, g, o_att, gates, gn_g, gn_b, w_prw, w_patt, w_out, ln1_g, ln1_b, tm):
    n = x.shape[0]
    tok = lambda w: pl.BlockSpec((tm, w), lambda i: (i, 0))
    vec = _resident((1, D_MODEL))
    mat = _resident((D_MODEL, D_MODEL))
    return pl.pallas_call(
        _merge_body,
        grid=(n // tm,),
        in_specs=[tok(D_MODEL), tok(D_MODEL), tok(D_MODEL), tok(D_MODEL), tok(D_MODEL), tok(D_MODEL),
                  tok(GATE_COLS), vec, vec, mat, mat, mat, vec, vec],
        out_specs=tok(D_MODEL),
        out_shape=jax.ShapeDtypeStruct((n, D_MODEL), F32),
        compiler_params=_params(("parallel",)),
    )(x, o_f, o_b, bonus, g, o_att, gates, gn_g, gn_b, w_prw, w_patt, w_out, ln1_g, ln1_b)


def _mlp_body(h_ref, w1_ref, w2_ref, g_ref, b_ref, o_ref):
    h = h_ref[...]
    u = jnp.maximum(jnp.dot(h.astype(BF16), w1_ref[...], preferred_element_type=F32), 0.0)
    ff = jnp.dot((u * u).astype(BF16), w2_ref[...], preferred_element_type=F32)
    o_ref[...] = _layernorm(ALPHA * h + ff, g_ref[...], b_ref[...], LN_EPS)


def _mlp(h, w1, w2, ln_g, ln_b, tm):
    n = h.shape[0]
    return pl.pallas_call(
        _mlp_body,
        grid=(n // tm,),
        in_specs=[pl.BlockSpec((tm, D_MODEL), lambda i: (i, 0)),
                  _resident((D_MODEL, D_FF)), _resident((D_FF, D_MODEL)),
                  _resident((1, D_MODEL)), _resident((1, D_MODEL))],
        out_specs=pl.BlockSpec((tm, D_MODEL), lambda i: (i, 0)),
        out_shape=jax.ShapeDtypeStruct((n, D_MODEL), F32),
        compiler_params=_params(("parallel",)),
    )(h, w1, w2, ln_g, ln_b)


def _axial_rope_tables(T):
    rows = T // GRID_W
    row = jnp.repeat(jnp.arange(rows, dtype=F32), GRID_W)
    col = jnp.tile(jnp.arange(GRID_W, dtype=F32), rows)
    half = ATT_HEAD // 2
    inv = ROPE_THETA ** (-jnp.arange(0, half, 2, dtype=F32) / half)
    ang = jnp.stack([row[:, None] * inv, col[:, None] * inv], axis=1)
    ang = jnp.broadcast_to(ang[:, :, None, :], (T, 2, 2, half // 2)).reshape(T, ATT_HEAD)
    return jnp.cos(ang), jnp.sin(ang)


def _pad_rank(w):
    z = jnp.zeros_like(w[0])
    return jnp.stack([jnp.concatenate([w[0], z], axis=0), jnp.concatenate([z, w[1]], axis=0)]).astype(BF16)


def _tile(n, pref):
    t = min(pref, n)
    while n % t:
        t //= 2
    return t


def _layer(x, p):
    B, T, D = x.shape
    n = B * T
    xf = x.reshape(n, D)
    cos, sin = _axial_rope_tables(T)
    z_rw, q_r, k_r, v_b, gates = _inproj(xf, p["w_rw"], p["w_qkv"], p["w_gate"], cos, sin,
                                         p["q_norm"], p["k_norm"], _tile(T, 256), T)

    phit, qp, psit, o1, bonus, g = _rwkv_prep(
        z_rw.reshape(B, T, RW_COLS), p["mu_prev"], p["mu_next"], p["w0"], p["w_up"], p["a0"], p["a_up"],
        p["g_up"], p["k_k"], p["k_a"], p["r_k"])
    o_f, o_b = _rwkv_scan(phit, qp, psit, o1, _tile(T // CHUNK, 2))

    score_bound = ATT_HEAD * Q_SCALE * jnp.max(jnp.abs(p["q_norm"])) * jnp.max(jnp.abs(p["k_norm"]))
    o_att = _attention(q_r.reshape(B, T, ATT_Q), k_r.reshape(B, T, ATT_KV), v_b.reshape(B, T, ATT_KV),
                       score_bound, _tile(T, 256), _tile(T, 2048))

    h = _merge(xf, o_f.reshape(n, D), o_b.reshape(n, D), bonus.reshape(n, D), g.reshape(n, D),
               o_att.reshape(n, D), gates, p["gn_g"], p["gn_b"], p["w_prw"], p["w_patt"], p["w_out"],
               p["ln1_g"], p["ln1_b"], _tile(n, 256))
    y = _mlp(h, p["w_ff1"], p["w_ff2"], p["ln2_g"], p["ln2_b"], _tile(n, 512))
    return y.reshape(B, T, D)


def kernel(x_prompt, x_sample, w_in, rw_mu_prev, rw_mu_next, rw_w0, rw_w_up, rw_a0, rw_a_up, rw_g_up, rw_k_k,
           rw_k_a, rw_r_k, rw_gn_g, rw_gn_b, q_norm, k_norm, w_proj_rwkv, w_proj_attn, w_out, ln1_g, ln1_b,
           w_ff1, w_ff2, ln2_g, ln2_b):
    def layer_params(l):
        w = w_in[l].astype(BF16)
        row = lambda a: a[l].reshape(1, -1)
        return dict(
            w_rw=w[:, :RW_COLS], w_qkv=w[:, RW_COLS:RW_COLS + QKV_COLS], w_gate=w[:, RW_COLS + QKV_COLS:],
            mu_prev=row(rw_mu_prev), mu_next=row(rw_mu_next),
            w0=rw_w0[l], w_up=_pad_rank(rw_w_up[l]), a0=rw_a0[l], a_up=_pad_rank(rw_a_up[l]),
            g_up=rw_g_up[l].astype(BF16), k_k=row(rw_k_k), k_a=row(rw_k_a), r_k=row(rw_r_k),
            gn_g=row(rw_gn_g), gn_b=row(rw_gn_b), q_norm=row(q_norm), k_norm=row(k_norm),
            w_prw=w_proj_rwkv[l].astype(BF16), w_patt=w_proj_attn[l].astype(BF16), w_out=w_out[l].astype(BF16),
            ln1_g=row(ln1_g), ln1_b=row(ln1_b), w_ff1=w_ff1[l].astype(BF16), w_ff2=w_ff2[l].astype(BF16),
            ln2_g=row(ln2_g), ln2_b=row(ln2_b))

    layers = [layer_params(l) for l in range(w_in.shape[0])]

    def trunk(x):
        for p in layers:
            x = _layer(x, p)
        return x

    return trunk(x_prompt), trunk(x_sample)
```

```python
import functools
import math

import jax
import jax.numpy as jnp
from jax import lax
from jax.experimental import pallas as pl
from jax.experimental.pallas import tpu as pltpu

F32 = jnp.float32
BF16 = jnp.bfloat16

D_MODEL = 1024
GRID_W = 64
RW_HEAD = 64
RW_WIDTH = 1024
W_RANK = 64
A_RANK = 64
G_RANK = 128
DECAY_SCALE = math.exp(-0.5)
GN_EPS = 64e-5
ATT_HEAD = 128
ATT_Q_HEADS = 8
ATT_KV_HEADS = 2
ATT_GROUP = ATT_Q_HEADS // ATT_KV_HEADS
ATT_Q = ATT_Q_HEADS * ATT_HEAD
ATT_KV = ATT_KV_HEADS * ATT_HEAD
ROPE_THETA = 10000.0
RMS_EPS = 1e-6
Q_SCALE = ATT_HEAD ** -0.5 * math.log2(math.e)
D_FF = 4 * D_MODEL
LN_EPS = 1e-5
DEPTH = 1
ALPHA = (2 * DEPTH) ** 0.25
RW_COLS = 3 * RW_WIDTH + 2 * W_RANK + 2 * A_RANK + G_RANK
QKV_COLS = ATT_Q + 2 * ATT_KV
GATE_COLS = 2 * D_MODEL

SCORE_LIMIT = 96.0
N_ATT_PASS = 1
CHUNK = 64
PAIR = 2 * RW_HEAD
N_PAIR = RW_WIDTH // PAIR
LANES = 128
SUBLANES = 8
VMEM_LIMIT = 56 * 1024 * 1024


def _params(sem):
    return pltpu.CompilerParams(dimension_semantics=sem, vmem_limit_bytes=VMEM_LIMIT)


def _dot(a, b):
    return jnp.dot(a.astype(BF16), b.astype(BF16), preferred_element_type=F32)


def _dot_nt(a, b):
    return lax.dot_general(a.astype(BF16), b.astype(BF16), (((1,), (1,)), ((), ())),
                           preferred_element_type=F32)


def _split(x):
    hi = x.astype(BF16)
    lo = (x - hi.astype(F32)).astype(BF16)
    return hi, lo


def _dot_split_lhs(x, w):
    hi, lo = _split(x)
    return (jnp.dot(hi, w, preferred_element_type=F32) + jnp.dot(lo, w, preferred_element_type=F32))


def _dot_split_rhs(w, x):
    hi, lo = _split(x)
    return (jnp.dot(w, hi, preferred_element_type=F32) + jnp.dot(w, lo, preferred_element_type=F32))


def _sigmoid(x):
    return 0.5 * jnp.tanh(0.5 * x) + 0.5


def _resident(shape):
    nd = len(shape)
    return pl.BlockSpec(shape, lambda *_: (0,) * nd, pipeline_mode=pl.Buffered(1))


def _seg_ones():
    r = lax.broadcasted_iota(jnp.int32, (LANES, LANES), 0) // RW_HEAD
    c = lax.broadcasted_iota(jnp.int32, (LANES, LANES), 1) // RW_HEAD
    return jnp.where(r == c, 1.0, 0.0).astype(BF16)


def _segsum64(x, ones_bd):
    parts = [_dot_split_lhs(x[:, j * LANES:(j + 1) * LANES], ones_bd) for j in range(x.shape[1] // LANES)]
    return jnp.concatenate(parts, axis=1)


def _norm_rope(xh, gain, cos, sin, first):
    ms = jnp.mean(xh * xh, axis=-1, keepdims=True)
    xn = xh * lax.rsqrt(ms + RMS_EPS) * gain
    rot = jnp.where(first, -pltpu.roll(xn, ATT_HEAD - ATT_HEAD // 4, 1), pltpu.roll(xn, ATT_HEAD // 4, 1))
    return xn * cos + rot * sin


def _inproj_body(x_ref, wrw_ref, wqkv_ref, wg_ref, cos_ref, sin_ref, qn_ref, kn_ref,
                 zrw_ref, q_ref, k_ref, v_ref, gate_ref):
    x = x_ref[...].astype(BF16)
    zrw_ref[...] = jnp.dot(x, wrw_ref[...], preferred_element_type=F32)
    gate_ref[...] = _sigmoid(jnp.dot(x, wg_ref[...], preferred_element_type=F32)).astype(BF16)
    qkv = jnp.dot(x, wqkv_ref[...], preferred_element_type=F32)
    cos = cos_ref[...]
    sin = sin_ref[...]
    lane = lax.broadcasted_iota(jnp.int32, cos.shape, 1)
    first = (lane % (ATT_HEAD // 2)) < (ATT_HEAD // 4)
    for h in range(ATT_Q_HEADS):
        sl = slice(h * ATT_HEAD, (h + 1) * ATT_HEAD)
        q_ref[:, sl] = (_norm_rope(qkv[:, sl], qn_ref[...], cos, sin, first) * Q_SCALE).astype(BF16)
    for h in range(ATT_KV_HEADS):
        sl = slice(h * ATT_HEAD, (h + 1) * ATT_HEAD)
        k_ref[:, sl] = _norm_rope(qkv[:, ATT_Q + h * ATT_HEAD:ATT_Q + (h + 1) * ATT_HEAD], kn_ref[...],
                                  cos, sin, first).astype(BF16)
    v_ref[...] = qkv[:, ATT_Q + ATT_KV:].astype(BF16)


def _inproj(x, w_rw, w_qkv, w_gate, cos, sin, q_norm, k_norm, tm, seq_len):
    n = x.shape[0]
    tiles_per_seq = seq_len // tm
    tok = lambda w: pl.BlockSpec((tm, w), lambda i: (i, 0))
    rope = pl.BlockSpec((tm, ATT_HEAD), lambda i: (i % tiles_per_seq, 0))
    return pl.pallas_call(
        _inproj_body,
        grid=(n // tm,),
        in_specs=[tok(D_MODEL),
                  _resident((D_MODEL, RW_COLS)), _resident((D_MODEL, QKV_COLS)), _resident((D_MODEL, GATE_COLS)),
                  rope, rope, _resident((1, ATT_HEAD)), _resident((1, ATT_HEAD))],
        out_specs=[tok(RW_COLS), tok(ATT_Q), tok(ATT_KV), tok(ATT_KV), tok(GATE_COLS)],
        out_shape=[jax.ShapeDtypeStruct((n, RW_COLS), F32),
                   jax.ShapeDtypeStruct((n, ATT_Q), BF16),
                   jax.ShapeDtypeStruct((n, ATT_KV), BF16),
                   jax.ShapeDtypeStruct((n, ATT_KV), BF16),
                   jax.ShapeDtypeStruct((n, GATE_COLS), BF16)],
        compiler_params=_params(("parallel",)),
    )(x, w_rw, w_qkv, w_gate, cos, sin, q_norm, k_norm)


def _block_diag(x, lane_lo):
    top = jnp.where(lane_lo, x, 0.0)
    bot = jnp.where(lane_lo, 0.0, x)
    return jnp.concatenate([top, bot], axis=0).astype(BF16)


def _pair_transpose(x, lane_lo):
    top = jnp.where(lane_lo, x, 0.0)
    bot = jnp.where(lane_lo, 0.0, x)
    xt = jnp.transpose(jnp.concatenate([top, bot], axis=0))
    return xt[:CHUNK] + xt[CHUNK:]


def _rwkv_prep_body(zc_ref, zp_ref, zn_ref, mup_ref, mun_ref, w0_ref, wup_ref, a0_ref, aup_ref, gup_ref,
                    kk_ref, ka_ref, rk_ref,
                    phit_ref, qp_ref, psit_ref, o1_ref, bonus_ref, g_ref):
    c_idx = pl.program_id(1)
    n_steps = pl.num_programs(1)
    L = CHUNK
    C = RW_WIDTH

    zc = zc_ref[0]
    R = zc.shape[0]
    n_sub = R // L
    row = lax.broadcasted_iota(jnp.int32, zc.shape, 0)
    prev_row = zp_ref[0][SUBLANES - 1:SUBLANES, :] * jnp.where(c_idx > 0, 1.0, 0.0)
    next_row = zn_ref[0][0:1, :] * jnp.where(c_idx < n_steps - 1, 1.0, 0.0)
    z_prev = jnp.where(row == 0, prev_row, pltpu.roll(zc, 1, 0))
    z_next = jnp.where(row == R - 1, next_row, pltpu.roll(zc, R - 1, 0))
    z = zc + mup_ref[...] * (z_prev - zc) + mun_ref[...] * (z_next - zc)

    r = z[:, 0:C]
    k = z[:, C:2 * C]
    v = z[:, 2 * C:3 * C]
    o_wd = 3 * C
    tw = jnp.tanh(z[:, o_wd:o_wd + 2 * W_RANK]).astype(BF16)
    ad = z[:, o_wd + 2 * W_RANK:o_wd + 2 * W_RANK + 2 * A_RANK].astype(BF16)
    gd = _sigmoid(z[:, o_wd + 2 * W_RANK + 2 * A_RANK:RW_COLS]).astype(BF16)
    g_ref[0] = jnp.dot(gd, gup_ref[...], preferred_element_type=F32).astype(BF16)

    ones_bd = _seg_ones()
    kappa = k * kk_ref[...]
    kh = kappa / jnp.maximum(jnp.sqrt(_segsum64(kappa * kappa, ones_bd)), 1e-12)

    ti = lax.broadcasted_iota(jnp.int32, (R, R), 0)
    si = lax.broadcasted_iota(jnp.int32, (R, R), 1)
    same_chunk = (ti // L) == (si // L)
    tp = lax.broadcasted_iota(jnp.int32, (L, PAIR), 0)
    lp = lax.broadcasted_iota(jnp.int32, (L, PAIR), 1)
    sp = lp % RW_HEAD
    lane_lo = lp < RW_HEAD
    eye_pair = jnp.where(sp == tp, 1.0, 0.0)

    kt_sum = jnp.zeros((R, C), F32)
    chains = []
    for d in range(2):
        if d == 0:
            tri = jnp.where(same_chunk & (si <= ti), 1.0, 0.0).astype(BF16)
            m_strict, m_incl = sp < tp, sp <= tp
        else:
            tri = jnp.where(same_chunk & (si >= ti), 1.0, 0.0).astype(BF16)
            m_strict, m_incl = sp > tp, sp >= tp
        wlog = w0_ref[d:d + 1, :] + jnp.dot(tw, wup_ref[d], preferred_element_type=F32)
        lw = -DECAY_SCALE * _sigmoid(wlog)
        a = _sigmoid(a0_ref[d:d + 1, :] + jnp.dot(ad, aup_ref[d], preferred_element_type=F32))
        kt = k * (1.0 + (a - 1.0) * ka_ref[...])
        kt_sum = kt_sum + kt
        ak = a * kh
        cum = _dot_split_rhs(tri, lw)
        last = [c * L + (L - 1 if d == 0 else 0) for c in range(n_sub)]
        ctot = jnp.concatenate([jnp.broadcast_to(cum[i:i + 1, :], (L, C)) for i in last], axis=0)
        excl = cum - lw
        a_t = -kh * jnp.exp(excl)
        q_t = r * jnp.exp(cum)
        inv = jnp.exp(-cum)
        b_t = ak * inv
        k_t = kt * inv
        suf = jnp.exp(ctot - cum)
        b_g = ak * suf
        k_g = kt * suf
        for c in range(n_sub):
            g_l = jnp.exp(cum[last[c]:last[c] + 1, :])
            rs = slice(c * L, (c + 1) * L)
            for p in range(N_PAIR):
                sl = slice(p * PAIR, (p + 1) * PAIR)
                chains.append(dict(c=c, d=d, p=p, at=a_t[rs, sl], qt=q_t[rs, sl], bt=b_t[rs, sl], kt=k_t[rs, sl],
                                   bg=b_g[rs, sl], kg=k_g[rs, sl], v=v[rs, sl], gl=g_l[:, sl],
                                   ms=m_strict, mi=m_incl))

    bonus_ref[0] = (_segsum64(r * kt_sum * rk_ref[...], ones_bd) * v).astype(BF16)

    bd = lambda x: _block_diag(x, lane_lo)
    bd2 = lambda x, y: jnp.concatenate([bd(x), bd(y)], axis=1)
    for ch in chains:
        aq = jnp.concatenate([ch["at"], ch["qt"]], axis=0)
        gram = _dot_nt(aq, jnp.concatenate([bd(ch["bt"]), bd(ch["kt"])], axis=0))
        ch["m_ab"] = jnp.where(ch["ms"], gram[:L, :PAIR], 0.0)
        ch["m_ak"] = jnp.where(ch["ms"], gram[:L, PAIR:], 0.0)
        ch["m_qb"] = jnp.where(ch["mi"], gram[L:, :PAIR], 0.0)
        ch["m_qk"] = jnp.where(ch["mi"], gram[L:, PAIR:], 0.0)
    for ch in chains:
        ch["bg_t"] = _pair_transpose(ch["bg"], lane_lo)
        ch["kg_t"] = _pair_transpose(ch["kg"], lane_lo)
    for ch in chains:
        ch["t"] = eye_pair + ch["m_ab"]
        ch["pw"] = _dot(ch["m_ab"], bd(ch["m_ab"]))
    for ch in chains:
        r3 = _dot(jnp.concatenate([ch["m_ak"], ch["m_qk"], ch["kg_t"]], axis=0), bd(ch["v"]))
        ch["p1"], ch["o1"], ch["psi"] = r3[:L], r3[L:2 * L], r3[2 * L:]
    for _ in range(4):
        for ch in chains:
            res = _dot(ch["pw"], bd2(ch["pw"], ch["t"]))
            ch["pw"] = res[:, :PAIR]
            ch["t"] = ch["t"] + res[:, PAIR:]
    for ch in chains:
        ch["t"] = ch["t"] + _dot(ch["pw"], bd(ch["t"]))
    for ch in chains:
        res = _dot(ch["t"], bd2(ch["at"], ch["p1"]))
        ch["a_p"], ch["u0"] = res[:, :PAIR], res[:, PAIR:]
    for ch in chains:
        res = _dot(jnp.concatenate([ch["m_qb"], ch["bg_t"]], axis=0), bd2(ch["a_p"], ch["u0"]))
        c, d, p = ch["c"], ch["d"], ch["p"]
        qp_ref[0, c, d, p] = (ch["qt"] + res[:L, :PAIR]).astype(BF16)
        o1_ref[0, c, d, p] = (ch["o1"] + res[:L, PAIR:]).astype(BF16)
        phit_ref[0, c, d, p] = (eye_pair * ch["gl"] + res[L:, :PAIR]).astype(BF16)
        psit_ref[0, c, d, p] = (ch["psi"] + res[L:, PAIR:]).astype(BF16)


def _rwkv_prep(z_rw, mu_prev, mu_next, w0, w_up_pad, a0, a_up_pad, g_up, k_k, k_a, r_k, n_sub):
    B, T, _ = z_rw.shape
    nc = T // CHUNK
    rows = n_sub * CHUNK
    nb8 = T // SUBLANES
    blk8 = rows // SUBLANES
    vec = lambda n: _resident((1, n))
    op_spec = pl.BlockSpec((1, n_sub, 2, N_PAIR, CHUNK, PAIR), lambda b, c: (b, c, 0, 0, 0, 0))
    tok_spec = pl.BlockSpec((1, rows, RW_WIDTH), lambda b, c: (b, c, 0))
    op_shape = (B, nc, 2, N_PAIR, CHUNK, PAIR)
    return pl.pallas_call(
        _rwkv_prep_body,
        grid=(B, nc // n_sub),
        in_specs=[pl.BlockSpec((1, rows, RW_COLS), lambda b, c: (b, c, 0)),
                  pl.BlockSpec((1, SUBLANES, RW_COLS), lambda b, c: (b, jnp.maximum(c * blk8 - 1, 0), 0)),
                  pl.BlockSpec((1, SUBLANES, RW_COLS), lambda b, c: (b, jnp.minimum((c + 1) * blk8, nb8 - 1), 0)),
                  vec(RW_COLS), vec(RW_COLS),
                  _resident((2, RW_WIDTH)), _resident((2, 2 * W_RANK, RW_WIDTH)),
                  _resident((2, RW_WIDTH)), _resident((2, 2 * A_RANK, RW_WIDTH)),
                  _resident((G_RANK, RW_WIDTH)),
                  vec(RW_WIDTH), vec(RW_WIDTH), vec(RW_WIDTH)],
        out_specs=[op_spec, op_spec, op_spec, op_spec, tok_spec, tok_spec],
        out_shape=[jax.ShapeDtypeStruct(op_shape, BF16), jax.ShapeDtypeStruct(op_shape, BF16),
                   jax.ShapeDtypeStruct(op_shape, BF16), jax.ShapeDtypeStruct(op_shape, BF16),
                   jax.ShapeDtypeStruct((B, T, RW_WIDTH), BF16), jax.ShapeDtypeStruct((B, T, RW_WIDTH), BF16)],
        compiler_params=_params(("parallel", "parallel")),
    )(z_rw, z_rw, z_rw, mu_prev, mu_next, w0, w_up_pad, a0, a_up_pad, g_up, k_k, k_a, r_k)


def _rwkv_scan_body(phif_ref, qpf_ref, psif_ref, o1f_ref, phib_ref, qpb_ref, psib_ref, o1b_ref,
                    of_ref, ob_ref, st_ref):
    @pl.when(pl.program_id(1) == 0)
    def _():
        st_ref[...] = jnp.zeros_like(st_ref)

    lane_lo = lax.broadcasted_iota(jnp.int32, (CHUNK, PAIR), 1) < RW_HEAD
    n_sub = phif_ref.shape[1]
    dirs = ((phif_ref, qpf_ref, psif_ref, o1f_ref, of_ref), (phib_ref, qpb_ref, psib_ref, o1b_ref, ob_ref))
    states = [[st_ref[d, p] for p in range(N_PAIR)] for d in range(2)]
    for step in range(n_sub):
        for d, (phi_ref, qp_ref, psi_ref, o1_ref, out_ref) in enumerate(dirs):
            c = step if d == 0 else n_sub - 1 - step
            for p in range(N_PAIR):
                lhs = jnp.concatenate([phi_ref[0, c, 0, p], qp_ref[0, c, 0, p]], axis=0)
                res = jnp.dot(lhs, _block_diag(states[d][p], lane_lo), preferred_element_type=F32)
                states[d][p] = res[:CHUNK] + psi_ref[0, c, 0, p].astype(F32)
                out_ref[0, c * CHUNK:(c + 1) * CHUNK, p * PAIR:(p + 1) * PAIR] = (
                    res[CHUNK:] + o1_ref[0, c, 0, p].astype(F32)).astype(out_ref.dtype)
    for d in range(2):
        for p in range(N_PAIR):
            st_ref[d, p] = states[d][p]


def _rwkv_scan(phit, qp, psit, o1, n_sub):
    B, nc = phit.shape[0], phit.shape[1]
    T = nc * CHUNK
    nblk = nc // n_sub
    blk = (1, n_sub, 1, N_PAIR, CHUNK, PAIR)
    fwd = pl.BlockSpec(blk, lambda b, j: (b, j, 0, 0, 0, 0))
    bwd = pl.BlockSpec(blk, lambda b, j: (b, nblk - 1 - j, 1, 0, 0, 0))
    return pl.pallas_call(
        _rwkv_scan_body,
        grid=(B, nblk),
        in_specs=[fwd, fwd, fwd, fwd, bwd, bwd, bwd, bwd],
        out_specs=[pl.BlockSpec((1, n_sub * CHUNK, RW_WIDTH), lambda b, j: (b, j, 0)),
                   pl.BlockSpec((1, n_sub * CHUNK, RW_WIDTH), lambda b, j: (b, nblk - 1 - j, 0))],
        out_shape=[jax.ShapeDtypeStruct((B, T, RW_WIDTH), BF16), jax.ShapeDtypeStruct((B, T, RW_WIDTH), BF16)],
        scratch_shapes=[pltpu.VMEM((2, N_PAIR, CHUNK, PAIR), F32)],
        compiler_params=_params(("parallel", "arbitrary")),
    )(phit, qp, psit, o1, phit, qp, psit, o1)


def _attn_body(q_ref, k_ref, v_ref, o_ref, m_ref, l_ref, acc_ref):
    ki = pl.program_id(3)

    @pl.when(ki == 0)
    def _():
        m_ref[...] = jnp.full_like(m_ref, -jnp.inf)
        l_ref[...] = jnp.zeros_like(l_ref)
        acc_ref[...] = jnp.zeros_like(acc_ref)

    q = q_ref[0]
    tq = q.shape[0]
    k = k_ref[0]
    v = v_ref[0]
    heads_per_pass = ATT_GROUP // N_ATT_PASS
    for hp in range(N_ATT_PASS):
        rows = slice(hp * heads_per_pass * tq, (hp + 1) * heads_per_pass * tq)
        qh = jnp.concatenate([q[:, g * ATT_HEAD:(g + 1) * ATT_HEAD]
                              for g in range(hp * heads_per_pass, (hp + 1) * heads_per_pass)], axis=0)
        s = lax.dot_general(qh, k, (((1,), (1,)), ((), ())), preferred_element_type=F32)
        m_prev = m_ref[rows, :]
        m_cur = jnp.maximum(m_prev, jnp.max(s, axis=-1, keepdims=True))
        alpha = jnp.exp2(m_prev - m_cur)
        p = jnp.exp2(s - m_cur[:, 0:1])
        l_ref[rows, :] = alpha * l_ref[rows, :] + jnp.sum(p, axis=-1, keepdims=True)
        acc_ref[rows, :] = alpha * acc_ref[rows, :] + jnp.dot(p.astype(BF16), v, preferred_element_type=F32)
        m_ref[rows, :] = m_cur

    @pl.when(ki == pl.num_programs(3) - 1)
    def _():
        o = acc_ref[...] / l_ref[...]
        for g in range(ATT_GROUP):
            o_ref[0, :, g * ATT_HEAD:(g + 1) * ATT_HEAD] = o[g * tq:(g + 1) * tq].astype(o_ref.dtype)


def _attn_unshifted_body(q_ref, k_ref, v_ref, o_ref, l_ref, acc_ref):
    ki = pl.program_id(3)

    @pl.when(ki == 0)
    def _():
        l_ref[...] = jnp.zeros_like(l_ref)
        acc_ref[...] = jnp.zeros_like(acc_ref)

    q = q_ref[0]
    tq = q.shape[0]
    q4 = jnp.concatenate([q[:, g * ATT_HEAD:(g + 1) * ATT_HEAD] for g in range(ATT_GROUP)], axis=0)
    s = lax.dot_general(q4, k_ref[0], (((1,), (1,)), ((), ())), preferred_element_type=F32)
    p = jnp.exp2(s)
    l_ref[...] = l_ref[...] + jnp.sum(p, axis=-1, keepdims=True)
    acc_ref[...] = acc_ref[...] + jnp.dot(p.astype(BF16), v_ref[0], preferred_element_type=F32)

    @pl.when(ki == pl.num_programs(3) - 1)
    def _():
        o = acc_ref[...] / l_ref[...]
        for g in range(ATT_GROUP):
            o_ref[0, :, g * ATT_HEAD:(g + 1) * ATT_HEAD] = o[g * tq:(g + 1) * tq].astype(o_ref.dtype)


def _attention_call(body, n_stat, q, k, v, tq, tk):
    B, T, _ = q.shape
    gw = ATT_GROUP * ATT_HEAD
    return pl.pallas_call(
        body,
        grid=(B, ATT_KV_HEADS, T // tq, T // tk),
        in_specs=[pl.BlockSpec((1, tq, gw), lambda b, h, i, j: (b, i, h)),
                  pl.BlockSpec((1, tk, ATT_HEAD), lambda b, h, i, j: (b, j, h)),
                  pl.BlockSpec((1, tk, ATT_HEAD), lambda b, h, i, j: (b, j, h))],
        out_specs=pl.BlockSpec((1, tq, gw), lambda b, h, i, j: (b, i, h)),
        out_shape=jax.ShapeDtypeStruct((B, T, ATT_Q), BF16),
        scratch_shapes=[pltpu.VMEM((ATT_GROUP * tq, ATT_HEAD), F32) for _ in range(n_stat)],
        compiler_params=_params(("parallel", "parallel", "parallel", "arbitrary")),
    )(q, k, v)


def _attention(q, k, v, score_bound, tq, tk):
    return lax.cond(score_bound <= SCORE_LIMIT,
                    lambda: _attention_call(_attn_unshifted_body, 2, q, k, v, tq, tk),
                    lambda: _attention_call(_attn_body, 3, q, k, v, tq, tk))


def _layernorm(y, g, b, eps):
    mu = jnp.mean(y, axis=-1, keepdims=True)
    d = y - mu
    var = jnp.mean(d * d, axis=-1, keepdims=True)
    return d * lax.rsqrt(var + eps) * g + b


def _merge_body(x_ref, of_ref, ob_ref, bonus_ref, g_ref, oatt_ref, gate_ref, gng_ref, gnb_ref,
                wprw_ref, wpatt_ref, wout_ref, ln1g_ref, ln1b_ref, h_ref):
    ones_bd = _seg_ones()
    o = of_ref[...].astype(F32) + ob_ref[...].astype(F32)
    inv_n = 1.0 / RW_HEAD
    mu = _segsum64(o, ones_bd) * inv_n
    d = o - mu
    var = _segsum64(d * d, ones_bd) * inv_n
    on = d * lax.rsqrt(var + GN_EPS) * gng_ref[...] + gnb_ref[...]
    o_rw = ((on + bonus_ref[...].astype(F32)) * g_ref[...].astype(F32)).astype(BF16)
    gates = gate_ref[...].astype(F32)
    merged = (gates[:, :D_MODEL] * jnp.dot(o_rw, wprw_ref[...], preferred_element_type=F32)
              + gates[:, D_MODEL:] * jnp.dot(oatt_ref[...], wpatt_ref[...], preferred_element_type=F32))
    mix = jnp.dot(merged.astype(BF16), wout_ref[...], preferred_element_type=F32)
    h_ref[...] = _layernorm(ALPHA * x_ref[...] + mix, ln1g_ref[...], ln1b_ref[...], LN_EPS)


def _merge(x, o_f, o_b, bonus, g, o_att, gates, gn_g, gn_b, w_prw, w_patt, w_out, ln1_g, ln1_b, tm):
    n = x.shape[0]
    tok = lambda w: pl.BlockSpec((tm, w), lambda i: (i, 0))
    vec = _resident((1, D_MODEL))
    mat = _resident((D_MODEL, D_MODEL))
    return pl.pallas_call(
        _merge_body,
        grid=(n // tm,),
        in_specs=[tok(D_MODEL), tok(D_MODEL), tok(D_MODEL), tok(D_MODEL), tok(D_MODEL), tok(D_MODEL),
                  tok(GATE_COLS), vec, vec, mat, mat, mat, vec, vec],
        out_specs=tok(D_MODEL),
        out_shape=jax.ShapeDtypeStruct((n, D_MODEL), F32),
        compiler_params=_params(("parallel",)),
    )(x, o_f, o_b, bonus, g, o_att, gates, gn_g, gn_b, w_prw, w_patt, w_out, ln1_g, ln1_b)


def _mlp_body(h_ref, w1_ref, w2_ref, g_ref, b_ref, o_ref):
    h = h_ref[...]
    u = jnp.maximum(jnp.dot(h.astype(BF16), w1_ref[...], preferred_element_type=F32), 0.0)
    ff = jnp.dot((u * u).astype(BF16), w2_ref[...], preferred_element_type=F32)
    o_ref[...] = _layernorm(ALPHA * h + ff, g_ref[...], b_ref[...], LN_EPS)


def _mlp(h, w1, w2, ln_g, ln_b, tm):
    n = h.shape[0]
    return pl.pallas_call(
        _mlp_body,
        grid=(n // tm,),
        in_specs=[pl.BlockSpec((tm, D_MODEL), lambda i: (i, 0)),
                  _resident((D_MODEL, D_FF)), _resident((D_FF, D_MODEL)),
                  _resident((1, D_MODEL)), _resident((1, D_MODEL))],
        out_specs=pl.BlockSpec((tm, D_MODEL), lambda i: (i, 0)),
        out_shape=jax.ShapeDtypeStruct((n, D_MODEL), F32),
        compiler_params=_params(("parallel",)),
    )(h, w1, w2, ln_g, ln_b)


def _axial_rope_tables(T):
    rows = T // GRID_W
    row = jnp.repeat(jnp.arange(rows, dtype=F32), GRID_W)
    col = jnp.tile(jnp.arange(GRID_W, dtype=F32), rows)
    half = ATT_HEAD // 2
    inv = ROPE_THETA ** (-jnp.arange(0, half, 2, dtype=F32) / half)
    ang = jnp.stack([row[:, None] * inv, col[:, None] * inv], axis=1)
    ang = jnp.broadcast_to(ang[:, :, None, :], (T, 2, 2, half // 2)).reshape(T, ATT_HEAD)
    return jnp.cos(ang), jnp.sin(ang)


def _pad_rank(w):
    z = jnp.zeros_like(w[0])
    return jnp.stack([jnp.concatenate([w[0], z], axis=0), jnp.concatenate([z, w[1]], axis=0)]).astype(BF16)


def _tile(n, pref):
    t = min(pref, n)
    while n % t:
        t //= 2
    return t


def _layer(x, p):
    B, T, D = x.shape
    n = B * T
    xf = x.reshape(n, D)
    cos, sin = _axial_rope_tables(T)
    z_rw, q_r, k_r, v_b, gates = _inproj(xf, p["w_rw"], p["w_qkv"], p["w_gate"], cos, sin,
                                         p["q_norm"], p["k_norm"], _tile(T, 256), T)

    phit, qp, psit, o1, bonus, g = _rwkv_prep(
        z_rw.reshape(B, T, RW_COLS), p["mu_prev"], p["mu_next"], p["w0"], p["w_up"], p["a0"], p["a_up"],
        p["g_up"], p["k_k"], p["k_a"], p["r_k"], _tile(T // CHUNK, 2))
    o_f, o_b = _rwkv_scan(phit, qp, psit, o1, _tile(T // CHUNK, 2))

    score_bound = ATT_HEAD * Q_SCALE * jnp.max(jnp.abs(p["q_norm"])) * jnp.max(jnp.abs(p["k_norm"]))
    o_att = _attention(q_r.reshape(B, T, ATT_Q), k_r.reshape(B, T, ATT_KV), v_b.reshape(B, T, ATT_KV),
                       score_bound, _tile(T, 512), _tile(T, 2048))

    h = _merge(xf, o_f.reshape(n, D), o_b.reshape(n, D), bonus.reshape(n, D), g.reshape(n, D),
               o_att.reshape(n, D), gates, p["gn_g"], p["gn_b"], p["w_prw"], p["w_patt"], p["w_out"],
               p["ln1_g"], p["ln1_b"], _tile(n, 256))
    y = _mlp(h, p["w_ff1"], p["w_ff2"], p["ln2_g"], p["ln2_b"], _tile(n, 512))
    return y.reshape(B, T, D)


def kernel(x_prompt, x_sample, w_in, rw_mu_prev, rw_mu_next, rw_w0, rw_w_up, rw_a0, rw_a_up, rw_g_up, rw_k_k,
           rw_k_a, rw_r_k, rw_gn_g, rw_gn_b, q_norm, k_norm, w_proj_rwkv, w_proj_attn, w_out, ln1_g, ln1_b,
           w_ff1, w_ff2, ln2_g, ln2_b):
    def layer_params(l):
        w = w_in[l].astype(BF16)
        row = lambda a: a[l].reshape(1, -1)
        return dict(
            w_rw=w[:, :RW_COLS], w_qkv=w[:, RW_COLS:RW_COLS + QKV_COLS], w_gate=w[:, RW_COLS + QKV_COLS:],
            mu_prev=row(rw_mu_prev), mu_next=row(rw_mu_next),
            w0=rw_w0[l], w_up=_pad_rank(rw_w_up[l]), a0=rw_a0[l], a_up=_pad_rank(rw_a_up[l]),
            g_up=rw_g_up[l].astype(BF16), k_k=row(rw_k_k), k_a=row(rw_k_a), r_k=row(rw_r_k),
            gn_g=row(rw_gn_g), gn_b=row(rw_gn_b), q_norm=row(q_norm), k_norm=row(k_norm),
            w_prw=w_proj_rwkv[l].astype(BF16), w_patt=w_proj_attn[l].astype(BF16), w_out=w_out[l].astype(BF16),
            ln1_g=row(ln1_g), ln1_b=row(ln1_b), w_ff1=w_ff1[l].astype(BF16), w_ff2=w_ff2[l].astype(BF16),
            ln2_g=row(ln2_g), ln2_b=row(ln2_b))

    layers = [layer_params(l) for l in range(w_in.shape[0])]

    def trunk(x):
        for p in layers:
            x = _layer(x, p)
        return x

    return trunk(x_prompt), trunk(x_sample)
```

```python
import functools
import math

import jax
import jax.numpy as jnp
from jax import lax
from jax.experimental import pallas as pl
from jax.experimental.pallas import tpu as pltpu

F32 = jnp.float32
BF16 = jnp.bfloat16

D_MODEL = 1024
GRID_W = 64
RW_HEAD = 64
RW_WIDTH = 1024
W_RANK = 64
A_RANK = 64
G_RANK = 128
DECAY_SCALE = math.exp(-0.5)
LOG2E = math.log2(math.e)
GN_EPS = 64e-5
ATT_HEAD = 128
ATT_Q_HEADS = 8
ATT_KV_HEADS = 2
ATT_GROUP = ATT_Q_HEADS // ATT_KV_HEADS
ATT_Q = ATT_Q_HEADS * ATT_HEAD
ATT_KV = ATT_KV_HEADS * ATT_HEAD
ROPE_THETA = 10000.0
RMS_EPS = 1e-6
Q_SCALE = ATT_HEAD ** -0.5 * math.log2(math.e)
D_FF = 4 * D_MODEL
LN_EPS = 1e-5
DEPTH = 1
ALPHA = (2 * DEPTH) ** 0.25
RW_COLS = 3 * RW_WIDTH + 2 * W_RANK + 2 * A_RANK + G_RANK
QKV_COLS = ATT_Q + 2 * ATT_KV
GATE_COLS = 2 * D_MODEL

SCORE_LIMIT = 96.0
N_ATT_PASS = 1
PREP_CHUNKS = 4
CHUNK = 64
PAIR = 2 * RW_HEAD
N_PAIR = RW_WIDTH // PAIR
LANES = 128
SUBLANES = 8
VMEM_LIMIT = 56 * 1024 * 1024


def _params(sem):
    return pltpu.CompilerParams(dimension_semantics=sem, vmem_limit_bytes=VMEM_LIMIT)


def _dot(a, b):
    return jnp.dot(a.astype(BF16), b.astype(BF16), preferred_element_type=F32)


def _dot_nt(a, b):
    return lax.dot_general(a.astype(BF16), b.astype(BF16), (((1,), (1,)), ((), ())),
                           preferred_element_type=F32)


def _split(x):
    hi = x.astype(BF16)
    lo = (x - hi.astype(F32)).astype(BF16)
    return hi, lo


def _dot_split_lhs(x, w):
    hi, lo = _split(x)
    return (jnp.dot(hi, w, preferred_element_type=F32) + jnp.dot(lo, w, preferred_element_type=F32))


def _dot_split_rhs(w, x):
    hi, lo = _split(x)
    return (jnp.dot(w, hi, preferred_element_type=F32) + jnp.dot(w, lo, preferred_element_type=F32))


def _sigmoid(x):
    return 0.5 * jnp.tanh(0.5 * x) + 0.5


def _resident(shape):
    nd = len(shape)
    return pl.BlockSpec(shape, lambda *_: (0,) * nd, pipeline_mode=pl.Buffered(1))


def _seg_ones():
    r = lax.broadcasted_iota(jnp.int32, (LANES, LANES), 0) // RW_HEAD
    c = lax.broadcasted_iota(jnp.int32, (LANES, LANES), 1) // RW_HEAD
    return jnp.where(r == c, 1.0, 0.0).astype(BF16)


def _segsum64(x, ones_bd, stack=True):
    rows, n = x.shape[0], x.shape[1] // LANES
    if not stack:
        return jnp.concatenate([_dot_split_lhs(x[:, j * LANES:(j + 1) * LANES], ones_bd) for j in range(n)], axis=1)
    hi, lo = _split(jnp.concatenate([x[:, j * LANES:(j + 1) * LANES] for j in range(n)], axis=0))
    s = jnp.dot(jnp.concatenate([hi, lo], axis=0), ones_bd, preferred_element_type=F32)
    s = s[:n * rows] + s[n * rows:]
    return jnp.concatenate([s[j * rows:(j + 1) * rows] for j in range(n)], axis=1)


def _norm_rope(xh, gain, cos, sin, first):
    ms = jnp.mean(xh * xh, axis=-1, keepdims=True)
    xn = xh * lax.rsqrt(ms + RMS_EPS) * gain
    rot = jnp.where(first, -pltpu.roll(xn, ATT_HEAD - ATT_HEAD // 4, 1), pltpu.roll(xn, ATT_HEAD // 4, 1))
    return xn * cos + rot * sin


def _inproj_body(x_ref, wrw_ref, wqkv_ref, wg_ref, cos_ref, sin_ref, qn_ref, kn_ref,
                 zrw_ref, q_ref, k_ref, v_ref, gate_ref):
    x = x_ref[...].astype(BF16)
    qkv = jnp.dot(x, wqkv_ref[...], preferred_element_type=F32)
    cos = cos_ref[...]
    sin = sin_ref[...]
    lane = lax.broadcasted_iota(jnp.int32, cos.shape, 1)
    first = (lane % (ATT_HEAD // 2)) < (ATT_HEAD // 4)
    for h in range(ATT_Q_HEADS):
        sl = slice(h * ATT_HEAD, (h + 1) * ATT_HEAD)
        q_ref[:, sl] = (_norm_rope(qkv[:, sl], qn_ref[...], cos, sin, first) * Q_SCALE).astype(BF16)
    for h in range(ATT_KV_HEADS):
        sl = slice(h * ATT_HEAD, (h + 1) * ATT_HEAD)
        k_ref[:, sl] = _norm_rope(qkv[:, ATT_Q + h * ATT_HEAD:ATT_Q + (h + 1) * ATT_HEAD], kn_ref[...],
                                  cos, sin, first).astype(BF16)
    v_ref[...] = qkv[:, ATT_Q + ATT_KV:].astype(BF16)
    gate_ref[...] = _sigmoid(jnp.dot(x, wg_ref[...], preferred_element_type=F32)).astype(BF16)
    zrw_ref[...] = jnp.dot(x, wrw_ref[...], preferred_element_type=F32)


def _inproj(x, w_rw, w_qkv, w_gate, cos, sin, q_norm, k_norm, tm, seq_len):
    n = x.shape[0]
    tiles_per_seq = seq_len // tm
    tok = lambda w: pl.BlockSpec((tm, w), lambda i: (i, 0))
    rope = pl.BlockSpec((tm, ATT_HEAD), lambda i: (i % tiles_per_seq, 0))
    return pl.pallas_call(
        _inproj_body,
        grid=(n // tm,),
        in_specs=[tok(D_MODEL),
                  _resident((D_MODEL, RW_COLS)), _resident((D_MODEL, QKV_COLS)), _resident((D_MODEL, GATE_COLS)),
                  rope, rope, _resident((1, ATT_HEAD)), _resident((1, ATT_HEAD))],
        out_specs=[tok(RW_COLS), tok(ATT_Q), tok(ATT_KV), tok(ATT_KV), tok(GATE_COLS)],
        out_shape=[jax.ShapeDtypeStruct((n, RW_COLS), F32),
                   jax.ShapeDtypeStruct((n, ATT_Q), BF16),
                   jax.ShapeDtypeStruct((n, ATT_KV), BF16),
                   jax.ShapeDtypeStruct((n, ATT_KV), BF16),
                   jax.ShapeDtypeStruct((n, GATE_COLS), BF16)],
        compiler_params=_params(("parallel",)),
    )(x, w_rw, w_qkv, w_gate, cos, sin, q_norm, k_norm)


def _block_diag(x, lane_lo):
    top = jnp.where(lane_lo, x, 0.0)
    bot = jnp.where(lane_lo, 0.0, x)
    return jnp.concatenate([top, bot], axis=0).astype(BF16)


def _pair_transpose(x, lane_lo):
    top = jnp.where(lane_lo, x, 0.0)
    bot = jnp.where(lane_lo, 0.0, x)
    xt = jnp.transpose(jnp.concatenate([top, bot], axis=0))
    return xt[:CHUNK] + xt[CHUNK:]


def _rwkv_prep_body(zc_ref, zp_ref, zn_ref, mup_ref, mun_ref, w0_ref, wup_ref, a0_ref, aup_ref, gup_ref,
                    kk_ref, ka_ref, rk_ref,
                    phit_ref, qp_ref, psit_ref, o1_ref, bonus_ref, g_ref):
    c_idx = pl.program_id(1)
    n_steps = pl.num_programs(1)
    L = CHUNK
    C = RW_WIDTH
    R = zc_ref.shape[1]
    n_sub = R // L

    tp = lax.broadcasted_iota(jnp.int32, (L, PAIR), 0)
    lp = lax.broadcasted_iota(jnp.int32, (L, PAIR), 1)
    sp = lp % RW_HEAD
    lane_lo = lp < RW_HEAD
    eye_pair = jnp.where(sp == tp, 1.0, 0.0)
    masks = ((sp < tp, sp <= tp), (sp > tp, sp >= tp))
    ti = lax.broadcasted_iota(jnp.int32, (L, L), 0)
    si = lax.broadcasted_iota(jnp.int32, (L, L), 1)
    tris = (jnp.where(si <= ti, 1.0, 0.0).astype(BF16), jnp.where(si >= ti, 1.0, 0.0).astype(BF16))
    ones_bd = _seg_ones()

    zc = zc_ref[0]
    row = lax.broadcasted_iota(jnp.int32, zc.shape, 0)
    prev_row = zp_ref[0][SUBLANES - 1:SUBLANES, :] * jnp.where(c_idx > 0, 1.0, 0.0)
    next_row = zn_ref[0][0:1, :] * jnp.where(c_idx < n_steps - 1, 1.0, 0.0)
    z_prev = jnp.where(row == 0, prev_row, pltpu.roll(zc, 1, 0))
    z_next = jnp.where(row == R - 1, next_row, pltpu.roll(zc, R - 1, 0))
    mu_p, mu_n = mup_ref[...], mun_ref[...]
    z = zc * (1.0 - mu_p - mu_n) + mu_p * z_prev + mu_n * z_next
    o_wd = 3 * C
    gd = _sigmoid(z[:, o_wd + 2 * W_RANK + 2 * A_RANK:RW_COLS]).astype(BF16)
    g_ref[0] = jnp.dot(gd, gup_ref[...], preferred_element_type=F32).astype(BF16)
    tw = jnp.tanh(z[:, o_wd:o_wd + 2 * W_RANK]).astype(BF16)
    ad = z[:, o_wd + 2 * W_RANK:o_wd + 2 * W_RANK + 2 * A_RANK].astype(BF16)
    w_logit = [w0_ref[d:d + 1, :] + jnp.dot(tw, wup_ref[d], preferred_element_type=F32) for d in range(2)]
    a_logit = [a0_ref[d:d + 1, :] + jnp.dot(ad, aup_ref[d], preferred_element_type=F32) for d in range(2)]

    def elementwise_pieces(c):
        rs = slice(c * L, (c + 1) * L)
        st = dict(ops=[None, None])

        def piece_kappa():
            st["r"], st["k"], st["v"] = z[rs, 0:C], z[rs, C:2 * C], z[rs, 2 * C:3 * C]
            kappa = st["k"] * kk_ref[...]
            st["kh"] = kappa / jnp.maximum(jnp.sqrt(_segsum64(kappa * kappa, ones_bd)), 1e-12)
            st["kka"] = st["k"] * ka_ref[...]

        def piece_gates(d):
            def run():
                st["lw"] = (-DECAY_SCALE * LOG2E) * _sigmoid(w_logit[d][rs, :])
                a = _sigmoid(a_logit[d][rs, :])
                st["kt"] = st["k"] + st["kka"] * (a - 1.0)
                st["kt_sum"] = st["kt"] if d == 0 else st["kt_sum"] + st["kt"]
                st["ak"] = a * st["kh"]
            return run

        def piece_decay(d):
            def run():
                lw = st["lw"]
                cum = _dot_split_rhs(tris[d], lw)
                ctot = cum[L - 1:L, :] if d == 0 else cum[0:1, :]
                g_l = jnp.exp2(ctot)
                inv = jnp.exp2(-cum)
                suf = g_l * inv
                st["ops"][d] = dict(at=-st["kh"] * jnp.exp2(cum - lw), qt=st["r"] * jnp.exp2(cum),
                                    bt=st["ak"] * inv, kt=st["kt"] * inv, bg=st["ak"] * suf, kg=st["kt"] * suf,
                                    gl=g_l)
            return run

        def piece_bonus():
            bonus_ref[0, rs, :] = (_segsum64(st["r"] * st["kt_sum"] * rk_ref[...], ones_bd) * st["v"]).astype(BF16)

        return st, [piece_kappa, piece_gates(0), piece_decay(0), piece_gates(1), piece_decay(1), piece_bonus]

    bd = lambda x: _block_diag(x, lane_lo)
    bd2 = lambda x, y: jnp.concatenate([bd(x), bd(y)], axis=1)

    def matmul_levels(c, st):
        chains = [dict(d=d, p=p, sl=slice(p * PAIR, (p + 1) * PAIR)) for d in range(2) for p in range(N_PAIR)]
        op = lambda ch, name: st["ops"][ch["d"]][name][:, ch["sl"]]

        def level_gram():
            for ch in chains:
                m_strict, m_incl = masks[ch["d"]]
                aq = jnp.concatenate([op(ch, "at"), op(ch, "qt")], axis=0)
                gram = _dot_nt(aq, jnp.concatenate([bd(op(ch, "bt")), bd(op(ch, "kt"))], axis=0))
                ch["m_ab"] = jnp.where(m_strict, gram[:L, :PAIR], 0.0)
                ch["m_ak"] = jnp.where(m_strict, gram[:L, PAIR:], 0.0)
                ch["m_qb"] = jnp.where(m_incl, gram[L:, :PAIR], 0.0)
                ch["m_qk"] = jnp.where(m_incl, gram[L:, PAIR:], 0.0)

        def level_transpose():
            for ch in chains:
                ch["bg_t"] = _pair_transpose(op(ch, "bg"), lane_lo)
                ch["kg_t"] = _pair_transpose(op(ch, "kg"), lane_lo)

        def level_square():
            for ch in chains:
                ch["t"] = eye_pair + ch["m_ab"]
                ch["pw"] = _dot(ch["m_ab"], bd(ch["m_ab"]))

        def level_values():
            for ch in chains:
                r3 = _dot(jnp.concatenate([ch["m_ak"], ch["m_qk"], ch["kg_t"]], axis=0), bd(st["v"][:, ch["sl"]]))
                ch["p1"], ch["o1"], ch["psi"] = r3[:L], r3[L:2 * L], r3[2 * L:]

        def level_iter():
            for ch in chains:
                res = _dot(ch["pw"], bd2(ch["pw"], ch["t"]))
                ch["pw"] = res[:, :PAIR]
                ch["t"] = ch["t"] + res[:, PAIR:]

        def level_last_factor():
            for ch in chains:
                ch["t"] = ch["t"] + _dot(ch["pw"], bd(ch["t"]))

        def level_solve():
            for ch in chains:
                res = _dot(ch["t"], bd2(op(ch, "at"), ch["p1"]))
                ch["a_p"], ch["u0"] = res[:, :PAIR], res[:, PAIR:]

        def level_out():
            for ch in chains:
                res = _dot(jnp.concatenate([ch["m_qb"], ch["bg_t"]], axis=0), bd2(ch["a_p"], ch["u0"]))
                d, p = ch["d"], ch["p"]
                qp_ref[0, c, d, p] = (op(ch, "qt") + res[:L, :PAIR]).astype(BF16)
                o1_ref[0, c, d, p] = (ch["o1"] + res[:L, PAIR:]).astype(BF16)
                phit_ref[0, c, d, p] = (eye_pair * op(ch, "gl") + res[L:, :PAIR]).astype(BF16)
                psit_ref[0, c, d, p] = (ch["psi"] + res[L:, PAIR:]).astype(BF16)

        return [level_gram, level_transpose, level_square, level_values, level_iter, level_iter, level_iter,
                level_iter, level_last_factor, level_solve, level_out]

    def interleave(levels, pieces):
        step = max(1, len(levels) // max(1, len(pieces)))
        pi = 0
        for n, level in enumerate(levels):
            level()
            if pi < len(pieces) and n % step == 0:
                pieces[pi]()
                pi += 1
        for piece in pieces[pi:]:
            piece()

    st_prev, pieces = elementwise_pieces(0)
    interleave([], pieces)
    for c in range(1, n_sub):
        st_cur, pieces = elementwise_pieces(c)
        interleave(matmul_levels(c - 1, st_prev), pieces)
        st_prev = st_cur
    interleave(matmul_levels(n_sub - 1, st_prev), [])


def _rwkv_prep(z_rw, mu_prev, mu_next, w0, w_up_pad, a0, a_up_pad, g_up, k_k, k_a, r_k, n_sub):
    B, T, _ = z_rw.shape
    nc = T // CHUNK
    rows = n_sub * CHUNK
    nb8 = T // SUBLANES
    blk8 = rows // SUBLANES
    vec = lambda n: _resident((1, n))
    op_spec = pl.BlockSpec((1, n_sub, 2, N_PAIR, CHUNK, PAIR), lambda b, c: (b, c, 0, 0, 0, 0))
    tok_spec = pl.BlockSpec((1, rows, RW_WIDTH), lambda b, c: (b, c, 0))
    op_shape = (B, nc, 2, N_PAIR, CHUNK, PAIR)
    return pl.pallas_call(
        _rwkv_prep_body,
        grid=(B, nc // n_sub),
        in_specs=[pl.BlockSpec((1, rows, RW_COLS), lambda b, c: (b, c, 0)),
                  pl.BlockSpec((1, SUBLANES, RW_COLS), lambda b, c: (b, jnp.maximum(c * blk8 - 1, 0), 0)),
                  pl.BlockSpec((1, SUBLANES, RW_COLS), lambda b, c: (b, jnp.minimum((c + 1) * blk8, nb8 - 1), 0)),
                  vec(RW_COLS), vec(RW_COLS),
                  _resident((2, RW_WIDTH)), _resident((2, 2 * W_RANK, RW_WIDTH)),
                  _resident((2, RW_WIDTH)), _resident((2, 2 * A_RANK, RW_WIDTH)),
                  _resident((G_RANK, RW_WIDTH)),
                  vec(RW_WIDTH), vec(RW_WIDTH), vec(RW_WIDTH)],
        out_specs=[op_spec, op_spec, op_spec, op_spec, tok_spec, tok_spec],
        out_shape=[jax.ShapeDtypeStruct(op_shape, BF16), jax.ShapeDtypeStruct(op_shape, BF16),
                   jax.ShapeDtypeStruct(op_shape, BF16), jax.ShapeDtypeStruct(op_shape, BF16),
                   jax.ShapeDtypeStruct((B, T, RW_WIDTH), BF16), jax.ShapeDtypeStruct((B, T, RW_WIDTH), BF16)],
        compiler_params=_params(("parallel", "parallel")),
    )(z_rw, z_rw, z_rw, mu_prev, mu_next, w0, w_up_pad, a0, a_up_pad, g_up, k_k, k_a, r_k)


def _rwkv_scan_body(phif_ref, qpf_ref, psif_ref, o1f_ref, phib_ref, qpb_ref, psib_ref, o1b_ref,
                    of_ref, ob_ref, st_ref):
    @pl.when(pl.program_id(1) == 0)
    def _():
        st_ref[...] = jnp.zeros_like(st_ref)

    lane_lo = lax.broadcasted_iota(jnp.int32, (CHUNK, PAIR), 1) < RW_HEAD
    n_sub = phif_ref.shape[1]
    dirs = ((phif_ref, qpf_ref, psif_ref, o1f_ref, of_ref), (phib_ref, qpb_ref, psib_ref, o1b_ref, ob_ref))
    states = [[st_ref[d, p] for p in range(N_PAIR)] for d in range(2)]
    for step in range(n_sub):
        for d, (phi_ref, qp_ref, psi_ref, o1_ref, out_ref) in enumerate(dirs):
            c = step if d == 0 else n_sub - 1 - step
            for p in range(N_PAIR):
                lhs = jnp.concatenate([phi_ref[0, c, 0, p], qp_ref[0, c, 0, p]], axis=0)
                res = jnp.dot(lhs, _block_diag(states[d][p], lane_lo), preferred_element_type=F32)
                states[d][p] = res[:CHUNK] + psi_ref[0, c, 0, p].astype(F32)
                out_ref[0, c * CHUNK:(c + 1) * CHUNK, p * PAIR:(p + 1) * PAIR] = (
                    res[CHUNK:] + o1_ref[0, c, 0, p].astype(F32)).astype(out_ref.dtype)
    for d in range(2):
        for p in range(N_PAIR):
            st_ref[d, p] = states[d][p]


def _rwkv_scan(phit, qp, psit, o1, n_sub):
    B, nc = phit.shape[0], phit.shape[1]
    T = nc * CHUNK
    nblk = nc // n_sub
    blk = (1, n_sub, 1, N_PAIR, CHUNK, PAIR)
    fwd = pl.BlockSpec(blk, lambda b, j: (b, j, 0, 0, 0, 0))
    bwd = pl.BlockSpec(blk, lambda b, j: (b, nblk - 1 - j, 1, 0, 0, 0))
    return pl.pallas_call(
        _rwkv_scan_body,
        grid=(B, nblk),
        in_specs=[fwd, fwd, fwd, fwd, bwd, bwd, bwd, bwd],
        out_specs=[pl.BlockSpec((1, n_sub * CHUNK, RW_WIDTH), lambda b, j: (b, j, 0)),
                   pl.BlockSpec((1, n_sub * CHUNK, RW_WIDTH), lambda b, j: (b, nblk - 1 - j, 0))],
        out_shape=[jax.ShapeDtypeStruct((B, T, RW_WIDTH), BF16), jax.ShapeDtypeStruct((B, T, RW_WIDTH), BF16)],
        scratch_shapes=[pltpu.VMEM((2, N_PAIR, CHUNK, PAIR), F32)],
        compiler_params=_params(("parallel", "arbitrary")),
    )(phit, qp, psit, o1, phit, qp, psit, o1)


def _attn_body(q_ref, k_ref, v_ref, o_ref, m_ref, l_ref, acc_ref):
    ki = pl.program_id(3)

    @pl.when(ki == 0)
    def _():
        m_ref[...] = jnp.full_like(m_ref, -jnp.inf)
        l_ref[...] = jnp.zeros_like(l_ref)
        acc_ref[...] = jnp.zeros_like(acc_ref)

    q = q_ref[0]
    tq = q.shape[0]
    k = k_ref[0]
    v = v_ref[0]
    heads_per_pass = ATT_GROUP // N_ATT_PASS
    for hp in range(N_ATT_PASS):
        rows = slice(hp * heads_per_pass * tq, (hp + 1) * heads_per_pass * tq)
        qh = jnp.concatenate([q[:, g * ATT_HEAD:(g + 1) * ATT_HEAD]
                              for g in range(hp * heads_per_pass, (hp + 1) * heads_per_pass)], axis=0)
        s = lax.dot_general(qh, k, (((1,), (1,)), ((), ())), preferred_element_type=F32)
        m_prev = m_ref[rows, :]
        m_cur = jnp.maximum(m_prev, jnp.max(s, axis=-1, keepdims=True))
        alpha = jnp.exp2(m_prev - m_cur)
        p = jnp.exp2(s - m_cur[:, 0:1])
        l_ref[rows, :] = alpha * l_ref[rows, :] + jnp.sum(p, axis=-1, keepdims=True)
        acc_ref[rows, :] = alpha * acc_ref[rows, :] + jnp.dot(p.astype(BF16), v, preferred_element_type=F32)
        m_ref[rows, :] = m_cur

    @pl.when(ki == pl.num_programs(3) - 1)
    def _():
        o = acc_ref[...] / l_ref[...]
        for g in range(ATT_GROUP):
            o_ref[0, :, g * ATT_HEAD:(g + 1) * ATT_HEAD] = o[g * tq:(g + 1) * tq].astype(o_ref.dtype)


def _attn_unshifted_body(q_ref, k_ref, v_ref, o_ref, l_ref, acc_ref):
    ki = pl.program_id(3)

    @pl.when(ki == 0)
    def _():
        l_ref[...] = jnp.zeros_like(l_ref)
        acc_ref[...] = jnp.zeros_like(acc_ref)

    q = q_ref[0]
    tq = q.shape[0]
    q4 = jnp.concatenate([q[:, g * ATT_HEAD:(g + 1) * ATT_HEAD] for g in range(ATT_GROUP)], axis=0)
    s = lax.dot_general(q4, k_ref[0], (((1,), (1,)), ((), ())), preferred_element_type=F32)
    p = jnp.exp2(s)
    l_ref[...] = l_ref[...] + jnp.sum(p, axis=-1, keepdims=True)
    acc_ref[...] = acc_ref[...] + jnp.dot(p.astype(BF16), v_ref[0], preferred_element_type=F32)

    @pl.when(ki == pl.num_programs(3) - 1)
    def _():
        o = acc_ref[...] / l_ref[...]
        for g in range(ATT_GROUP):
            o_ref[0, :, g * ATT_HEAD:(g + 1) * ATT_HEAD] = o[g * tq:(g + 1) * tq].astype(o_ref.dtype)


def _attention_call(body, n_stat, q, k, v, tq, tk):
    B, T, _ = q.shape
    gw = ATT_GROUP * ATT_HEAD
    return pl.pallas_call(
        body,
        grid=(B, ATT_KV_HEADS, T // tq, T // tk),
        in_specs=[pl.BlockSpec((1, tq, gw), lambda b, h, i, j: (b, i, h)),
                  pl.BlockSpec((1, tk, ATT_HEAD), lambda b, h, i, j: (b, j, h)),
                  pl.BlockSpec((1, tk, ATT_HEAD), lambda b, h, i, j: (b, j, h))],
        out_specs=pl.BlockSpec((1, tq, gw), lambda b, h, i, j: (b, i, h)),
        out_shape=jax.ShapeDtypeStruct((B, T, ATT_Q), BF16),
        scratch_shapes=[pltpu.VMEM((ATT_GROUP * tq, ATT_HEAD), F32) for _ in range(n_stat)],
        compiler_params=_params(("parallel", "parallel", "parallel", "arbitrary")),
    )(q, k, v)


def _attention(q, k, v, score_bound, tq, tk):
    return lax.cond(score_bound <= SCORE_LIMIT,
                    lambda: _attention_call(_attn_unshifted_body, 2, q, k, v, tq, tk),
                    lambda: _attention_call(_attn_body, 3, q, k, v, tq, tk))


def _layernorm(y, g, b, eps):
    mu = jnp.mean(y, axis=-1, keepdims=True)
    d = y - mu
    var = jnp.mean(d * d, axis=-1, keepdims=True)
    return d * lax.rsqrt(var + eps) * g + b


def _merge_body(x_ref, of_ref, ob_ref, bonus_ref, g_ref, oatt_ref, gate_ref, gng_ref, gnb_ref,
                wprw_ref, wpatt_ref, wout_ref, ln1g_ref, ln1b_ref, h_ref):
    ones_bd = _seg_ones()
    gates = gate_ref[...].astype(F32)
    att = gates[:, D_MODEL:] * jnp.dot(oatt_ref[...], wpatt_ref[...], preferred_element_type=F32)
    o = of_ref[...].astype(F32) + ob_ref[...].astype(F32)
    inv_n = 1.0 / RW_HEAD
    mu = _segsum64(o, ones_bd, stack=False) * inv_n
    d = o - mu
    var = _segsum64(d * d, ones_bd, stack=False) * inv_n
    on = d * lax.rsqrt(var + GN_EPS) * gng_ref[...] + gnb_ref[...]
    o_rw = ((on + bonus_ref[...].astype(F32)) * g_ref[...].astype(F32)).astype(BF16)
    merged = gates[:, :D_MODEL] * jnp.dot(o_rw, wprw_ref[...], preferred_element_type=F32) + att
    mix = jnp.dot(merged.astype(BF16), wout_ref[...], preferred_element_type=F32)
    h_ref[...] = _layernorm(ALPHA * x_ref[...] + mix, ln1g_ref[...], ln1b_ref[...], LN_EPS)


def _merge(x, o_f, o_b, bonus, g, o_att, gates, gn_g, gn_b, w_prw, w_patt, w_out, ln1_g, ln1_b, tm):
    n = x.shape[0]
    tok = lambda w: pl.BlockSpec((tm, w), lambda i: (i, 0))
    vec = _resident((1, D_MODEL))
    mat = _resident((D_MODEL, D_MODEL))
    return pl.pallas_call(
        _merge_body,
        grid=(n // tm,),
        in_specs=[tok(D_MODEL), tok(D_MODEL), tok(D_MODEL), tok(D_MODEL), tok(D_MODEL), tok(D_MODEL),
                  tok(GATE_COLS), vec, vec, mat, mat, mat, vec, vec],
        out_specs=tok(D_MODEL),
        out_shape=jax.ShapeDtypeStruct((n, D_MODEL), F32),
        compiler_params=_params(("parallel",)),
    )(x, o_f, o_b, bonus, g, o_att, gates, gn_g, gn_b, w_prw, w_patt, w_out, ln1_g, ln1_b)


def _mlp_body(h_ref, w1_ref, w2_ref, g_ref, b_ref, o_ref):
    h = h_ref[...]
    u = jnp.maximum(jnp.dot(h.astype(BF16), w1_ref[...], preferred_element_type=F32), 0.0)
    ff = jnp.dot((u * u).astype(BF16), w2_ref[...], preferred_element_type=F32)
    o_ref[...] = _layernorm(ALPHA * h + ff, g_ref[...], b_ref[...], LN_EPS)


def _mlp(h, w1, w2, ln_g, ln_b, tm):
    n = h.shape[0]
    return pl.pallas_call(
        _mlp_body,
        grid=(n // tm,),
        in_specs=[pl.BlockSpec((tm, D_MODEL), lambda i: (i, 0)),
                  _resident((D_MODEL, D_FF)), _resident((D_FF, D_MODEL)),
                  _resident((1, D_MODEL)), _resident((1, D_MODEL))],
        out_specs=pl.BlockSpec((tm, D_MODEL), lambda i: (i, 0)),
        out_shape=jax.ShapeDtypeStruct((n, D_MODEL), F32),
        compiler_params=_params(("parallel",)),
    )(h, w1, w2, ln_g, ln_b)


def _axial_rope_tables(T):
    rows = T // GRID_W
    row = jnp.repeat(jnp.arange(rows, dtype=F32), GRID_W)
    col = jnp.tile(jnp.arange(GRID_W, dtype=F32), rows)
    half = ATT_HEAD // 2
    inv = ROPE_THETA ** (-jnp.arange(0, half, 2, dtype=F32) / half)
    ang = jnp.stack([row[:, None] * inv, col[:, None] * inv], axis=1)
    ang = jnp.broadcast_to(ang[:, :, None, :], (T, 2, 2, half // 2)).reshape(T, ATT_HEAD)
    return jnp.cos(ang), jnp.sin(ang)


def _pad_rank(w):
    z = jnp.zeros_like(w[0])
    return jnp.stack([jnp.concatenate([w[0], z], axis=0), jnp.concatenate([z, w[1]], axis=0)]).astype(BF16)


def _tile(n, pref):
    t = min(pref, n)
    while n % t:
        t //= 2
    return t


def _layer(x, p):
    B, T, D = x.shape
    n = B * T
    xf = x.reshape(n, D)
    cos, sin = _axial_rope_tables(T)
    z_rw, q_r, k_r, v_b, gates = _inproj(xf, p["w_rw"], p["w_qkv"], p["w_gate"], cos, sin,
                                         p["q_norm"], p["k_norm"], _tile(T, 256), T)

    phit, qp, psit, o1, bonus, g = _rwkv_prep(
        z_rw.reshape(B, T, RW_COLS), p["mu_prev"], p["mu_next"], p["w0"], p["w_up"], p["a0"], p["a_up"],
        p["g_up"], p["k_k"], p["k_a"], p["r_k"], _tile(T // CHUNK, PREP_CHUNKS))
    o_f, o_b = _rwkv_scan(phit, qp, psit, o1, _tile(T // CHUNK, 2))

    score_bound = ATT_HEAD * Q_SCALE * jnp.max(jnp.abs(p["q_norm"])) * jnp.max(jnp.abs(p["k_norm"]))
    o_att = _attention(q_r.reshape(B, T, ATT_Q), k_r.reshape(B, T, ATT_KV), v_b.reshape(B, T, ATT_KV),
                       score_bound, _tile(T, 512), _tile(T, 2048))

    h = _merge(xf, o_f.reshape(n, D), o_b.reshape(n, D), bonus.reshape(n, D), g.reshape(n, D),
               o_att.reshape(n, D), gates, p["gn_g"], p["gn_b"], p["w_prw"], p["w_patt"], p["w_out"],
               p["ln1_g"], p["ln1_b"], _tile(n, 256))
    y = _mlp(h, p["w_ff1"], p["w_ff2"], p["ln2_g"], p["ln2_b"], _tile(n, 512))
    return y.reshape(B, T, D)


def kernel(x_prompt, x_sample, w_in, rw_mu_prev, rw_mu_next, rw_w0, rw_w_up, rw_a0, rw_a_up, rw_g_up, rw_k_k,
           rw_k_a, rw_r_k, rw_gn_g, rw_gn_b, q_norm, k_norm, w_proj_rwkv, w_proj_attn, w_out, ln1_g, ln1_b,
           w_ff1, w_ff2, ln2_g, ln2_b):
    def layer_params(l):
        w = w_in[l].astype(BF16)
        row = lambda a: a[l].reshape(1, -1)
        return dict(
            w_rw=w[:, :RW_COLS], w_qkv=w[:, RW_COLS:RW_COLS + QKV_COLS], w_gate=w[:, RW_COLS + QKV_COLS:],
            mu_prev=row(rw_mu_prev), mu_next=row(rw_mu_next),
            w0=rw_w0[l], w_up=_pad_rank(rw_w_up[l]), a0=rw_a0[l], a_up=_pad_rank(rw_a_up[l]),
            g_up=rw_g_up[l].astype(BF16), k_k=row(rw_k_k), k_a=row(rw_k_a), r_k=row(rw_r_k),
            gn_g=row(rw_gn_g), gn_b=row(rw_gn_b), q_norm=row(q_norm), k_norm=row(k_norm),
            w_prw=w_proj_rwkv[l].astype(BF16), w_patt=w_proj_attn[l].astype(BF16), w_out=w_out[l].astype(BF16),
            ln1_g=row(ln1_g), ln1_b=row(ln1_b), w_ff1=w_ff1[l].astype(BF16), w_ff2=w_ff2[l].astype(BF16),
            ln2_g=row(ln2_g), ln2_b=row(ln2_b))

    layers = [layer_params(l) for l in range(w_in.shape[0])]

    def trunk(x):
        for p in layers:
            x = _layer(x, p)
        return x

    return trunk(x_prompt), trunk(x_sample)
```

```python
import functools
import math

import jax
import jax.numpy as jnp
from jax import lax
from jax.experimental import pallas as pl
from jax.experimental.pallas import tpu as pltpu

F32 = jnp.float32
BF16 = jnp.bfloat16

D_MODEL = 1024
GRID_W = 64
RW_HEAD = 64
RW_WIDTH = 1024
W_RANK = 64
A_RANK = 64
G_RANK = 128
DECAY_SCALE = math.exp(-0.5)
LOG2E = math.log2(math.e)
GN_EPS = 64e-5
ATT_HEAD = 128
ATT_Q_HEADS = 8
ATT_KV_HEADS = 2
ATT_GROUP = ATT_Q_HEADS // ATT_KV_HEADS
ATT_Q = ATT_Q_HEADS * ATT_HEAD
ATT_KV = ATT_KV_HEADS * ATT_HEAD
ROPE_THETA = 10000.0
RMS_EPS = 1e-6
Q_SCALE = ATT_HEAD ** -0.5 * math.log2(math.e)
D_FF = 4 * D_MODEL
LN_EPS = 1e-5
DEPTH = 1
ALPHA = (2 * DEPTH) ** 0.25
RW_COLS = 3 * RW_WIDTH + 2 * W_RANK + 2 * A_RANK + G_RANK
QKV_COLS = ATT_Q + 2 * ATT_KV
GATE_COLS = 2 * D_MODEL

SCORE_LIMIT = 96.0
N_ATT_PASS = 1
PREP_CHUNKS = 4
CHUNK = 64
PAIR = 2 * RW_HEAD
N_PAIR = RW_WIDTH // PAIR
LANES = 128
SUBLANES = 8
VMEM_LIMIT = 56 * 1024 * 1024


def _params(sem):
    return pltpu.CompilerParams(dimension_semantics=sem, vmem_limit_bytes=VMEM_LIMIT)


def _dot(a, b):
    return jnp.dot(a.astype(BF16), b.astype(BF16), preferred_element_type=F32)


def _dot_nt(a, b):
    return lax.dot_general(a.astype(BF16), b.astype(BF16), (((1,), (1,)), ((), ())),
                           preferred_element_type=F32)


def _split(x):
    hi = x.astype(BF16)
    lo = (x - hi.astype(F32)).astype(BF16)
    return hi, lo


def _dot_split_lhs(x, w):
    hi, lo = _split(x)
    return (jnp.dot(hi, w, preferred_element_type=F32) + jnp.dot(lo, w, preferred_element_type=F32))


def _dot_split_rhs(w, x):
    hi, lo = _split(x)
    return (jnp.dot(w, hi, preferred_element_type=F32) + jnp.dot(w, lo, preferred_element_type=F32))


def _sigmoid(x):
    return 0.5 * jnp.tanh(0.5 * x) + 0.5


def _resident(shape):
    nd = len(shape)
    return pl.BlockSpec(shape, lambda *_: (0,) * nd, pipeline_mode=pl.Buffered(1))


def _seg_ones():
    r = lax.broadcasted_iota(jnp.int32, (LANES, LANES), 0) // RW_HEAD
    c = lax.broadcasted_iota(jnp.int32, (LANES, LANES), 1) // RW_HEAD
    return jnp.where(r == c, 1.0, 0.0).astype(BF16)


def _segsum64(x, ones_bd, stack=True):
    rows, n = x.shape[0], x.shape[1] // LANES
    if not stack:
        return jnp.concatenate([_dot_split_lhs(x[:, j * LANES:(j + 1) * LANES], ones_bd) for j in range(n)], axis=1)
    hi, lo = _split(jnp.concatenate([x[:, j * LANES:(j + 1) * LANES] for j in range(n)], axis=0))
    s = jnp.dot(jnp.concatenate([hi, lo], axis=0), ones_bd, preferred_element_type=F32)
    s = s[:n * rows] + s[n * rows:]
    return jnp.concatenate([s[j * rows:(j + 1) * rows] for j in range(n)], axis=1)


def _norm_rope(xh, gain, cos, sin, first):
    ms = jnp.mean(xh * xh, axis=-1, keepdims=True)
    xn = xh * lax.rsqrt(ms + RMS_EPS) * gain
    rot = jnp.where(first, -pltpu.roll(xn, ATT_HEAD - ATT_HEAD // 4, 1), pltpu.roll(xn, ATT_HEAD // 4, 1))
    return xn * cos + rot * sin


def _inproj_body(x_ref, wrw_ref, wqkv_ref, wg_ref, cos_ref, sin_ref, qn_ref, kn_ref,
                 zrw_ref, q_ref, k_ref, v_ref, gate_ref):
    x = x_ref[...].astype(BF16)
    qkv = jnp.dot(x, wqkv_ref[...], preferred_element_type=F32)
    cos = cos_ref[...]
    sin = sin_ref[...]
    lane = lax.broadcasted_iota(jnp.int32, cos.shape, 1)
    first = (lane % (ATT_HEAD // 2)) < (ATT_HEAD // 4)
    for h in range(ATT_Q_HEADS):
        sl = slice(h * ATT_HEAD, (h + 1) * ATT_HEAD)
        q_ref[:, sl] = (_norm_rope(qkv[:, sl], qn_ref[...], cos, sin, first) * Q_SCALE).astype(BF16)
    for h in range(ATT_KV_HEADS):
        sl = slice(h * ATT_HEAD, (h + 1) * ATT_HEAD)
        k_ref[:, sl] = _norm_rope(qkv[:, ATT_Q + h * ATT_HEAD:ATT_Q + (h + 1) * ATT_HEAD], kn_ref[...],
                                  cos, sin, first).astype(BF16)
    v_ref[...] = qkv[:, ATT_Q + ATT_KV:].astype(BF16)
    gate_ref[...] = _sigmoid(jnp.dot(x, wg_ref[...], preferred_element_type=F32)).astype(BF16)
    zrw_ref[...] = jnp.dot(x, wrw_ref[...], preferred_element_type=F32)


def _inproj(x, w_rw, w_qkv, w_gate, cos, sin, q_norm, k_norm, tm, seq_len):
    n = x.shape[0]
    tiles_per_seq = seq_len // tm
    tok = lambda w: pl.BlockSpec((tm, w), lambda i: (i, 0))
    rope = pl.BlockSpec((tm, ATT_HEAD), lambda i: (i % tiles_per_seq, 0))
    return pl.pallas_call(
        _inproj_body,
        grid=(n // tm,),
        in_specs=[tok(D_MODEL),
                  _resident((D_MODEL, RW_COLS)), _resident((D_MODEL, QKV_COLS)), _resident((D_MODEL, GATE_COLS)),
                  rope, rope, _resident((1, ATT_HEAD)), _resident((1, ATT_HEAD))],
        out_specs=[tok(RW_COLS), tok(ATT_Q), tok(ATT_KV), tok(ATT_KV), tok(GATE_COLS)],
        out_shape=[jax.ShapeDtypeStruct((n, RW_COLS), F32),
                   jax.ShapeDtypeStruct((n, ATT_Q), BF16),
                   jax.ShapeDtypeStruct((n, ATT_KV), BF16),
                   jax.ShapeDtypeStruct((n, ATT_KV), BF16),
                   jax.ShapeDtypeStruct((n, GATE_COLS), BF16)],
        compiler_params=_params(("parallel",)),
    )(x, w_rw, w_qkv, w_gate, cos, sin, q_norm, k_norm)


def _block_diag(x, lane_lo):
    top = jnp.where(lane_lo, x, 0.0)
    bot = jnp.where(lane_lo, 0.0, x)
    return jnp.concatenate([top, bot], axis=0).astype(BF16)


def _pair_transpose(x, lane_lo):
    top = jnp.where(lane_lo, x, 0.0)
    bot = jnp.where(lane_lo, 0.0, x)
    xt = jnp.transpose(jnp.concatenate([top, bot], axis=0))
    return xt[:CHUNK] + xt[CHUNK:]


def _rwkv_prep_body(zc_ref, zp_ref, zn_ref, mup_ref, mun_ref, w0_ref, wup_ref, a0_ref, aup_ref, gup_ref,
                    kk_ref, ka_ref, rk_ref,
                    phit_ref, qp_ref, psit_ref, o1_ref, bonus_ref, g_ref):
    c_idx = pl.program_id(1)
    n_steps = pl.num_programs(1)
    L = CHUNK
    C = RW_WIDTH
    R = zc_ref.shape[1]
    n_sub = R // L

    tp = lax.broadcasted_iota(jnp.int32, (L, PAIR), 0)
    lp = lax.broadcasted_iota(jnp.int32, (L, PAIR), 1)
    sp = lp % RW_HEAD
    lane_lo = lp < RW_HEAD
    eye_pair = jnp.where(sp == tp, 1.0, 0.0)
    masks = ((sp < tp, sp <= tp), (sp > tp, sp >= tp))
    ti = lax.broadcasted_iota(jnp.int32, (L, L), 0)
    si = lax.broadcasted_iota(jnp.int32, (L, L), 1)
    tris = (jnp.where(si <= ti, 1.0, 0.0).astype(BF16), jnp.where(si >= ti, 1.0, 0.0).astype(BF16))
    ones_bd = _seg_ones()

    zc = zc_ref[0]
    row = lax.broadcasted_iota(jnp.int32, zc.shape, 0)
    prev_row = zp_ref[0][SUBLANES - 1:SUBLANES, :] * jnp.where(c_idx > 0, 1.0, 0.0)
    next_row = zn_ref[0][0:1, :] * jnp.where(c_idx < n_steps - 1, 1.0, 0.0)
    z_prev = jnp.where(row == 0, prev_row, pltpu.roll(zc, 1, 0))
    z_next = jnp.where(row == R - 1, next_row, pltpu.roll(zc, R - 1, 0))
    mu_p, mu_n = mup_ref[...], mun_ref[...]
    z = zc * (1.0 - mu_p - mu_n) + mu_p * z_prev + mu_n * z_next
    o_wd = 3 * C
    gd = _sigmoid(z[:, o_wd + 2 * W_RANK + 2 * A_RANK:RW_COLS]).astype(BF16)
    g_ref[0] = jnp.dot(gd, gup_ref[...], preferred_element_type=F32).astype(BF16)
    tw = jnp.tanh(z[:, o_wd:o_wd + 2 * W_RANK]).astype(BF16)
    ad = z[:, o_wd + 2 * W_RANK:o_wd + 2 * W_RANK + 2 * A_RANK].astype(BF16)
    w_logit = [w0_ref[d:d + 1, :] + jnp.dot(tw, wup_ref[d], preferred_element_type=F32) for d in range(2)]
    a_logit = [a0_ref[d:d + 1, :] + jnp.dot(ad, aup_ref[d], preferred_element_type=F32) for d in range(2)]

    def elementwise_pieces(c):
        rs = slice(c * L, (c + 1) * L)
        st = dict(ops=[None, None])

        def piece_kappa():
            st["r"], st["k"], st["v"] = z[rs, 0:C], z[rs, C:2 * C], z[rs, 2 * C:3 * C]
            kappa = st["k"] * kk_ref[...]
            st["kh"] = kappa / jnp.maximum(jnp.sqrt(_segsum64(kappa * kappa, ones_bd)), 1e-12)
            st["kka"] = st["k"] * ka_ref[...]

        def piece_gates(d):
            def run():
                st["lw"] = (-DECAY_SCALE * LOG2E) * _sigmoid(w_logit[d][rs, :])
                a = _sigmoid(a_logit[d][rs, :])
                st["kt"] = st["k"] + st["kka"] * (a - 1.0)
                st["kt_sum"] = st["kt"] if d == 0 else st["kt_sum"] + st["kt"]
                st["ak"] = a * st["kh"]
            return run

        def piece_decay(d):
            def run():
                lw = st["lw"]
                cum = _dot_split_rhs(tris[d], lw)
                ctot = cum[L - 1:L, :] if d == 0 else cum[0:1, :]
                g_l = jnp.exp2(ctot)
                inv = jnp.exp2(-cum)
                suf = g_l * inv
                st["ops"][d] = dict(at=-st["kh"] * jnp.exp2(cum - lw), qt=st["r"] * jnp.exp2(cum),
                                    bt=st["ak"] * inv, kt=st["kt"] * inv, bg=st["ak"] * suf, kg=st["kt"] * suf,
                                    gl=g_l)
            return run

        def piece_bonus():
            bonus_ref[0, rs, :] = (_segsum64(st["r"] * st["kt_sum"] * rk_ref[...], ones_bd) * st["v"]).astype(BF16)

        return st, [piece_kappa, piece_gates(0), piece_decay(0), piece_gates(1), piece_decay(1), piece_bonus]

    bd = lambda x: _block_diag(x, lane_lo)
    bd2 = lambda x, y: jnp.concatenate([bd(x), bd(y)], axis=1)

    def matmul_levels(c, st):
        chains = [dict(d=d, p=p, sl=slice(p * PAIR, (p + 1) * PAIR)) for d in range(2) for p in range(N_PAIR)]
        op = lambda ch, name: st["ops"][ch["d"]][name][:, ch["sl"]]

        def level_gram():
            for ch in chains:
                m_strict, m_incl = masks[ch["d"]]
                aq = jnp.concatenate([op(ch, "at"), op(ch, "qt")], axis=0)
                gram = _dot_nt(aq, jnp.concatenate([bd(op(ch, "bt")), bd(op(ch, "kt"))], axis=0))
                ch["m_ab"] = jnp.where(m_strict, gram[:L, :PAIR], 0.0)
                ch["m_ak"] = jnp.where(m_strict, gram[:L, PAIR:], 0.0)
                ch["m_qb"] = jnp.where(m_incl, gram[L:, :PAIR], 0.0)
                ch["m_qk"] = jnp.where(m_incl, gram[L:, PAIR:], 0.0)

        def level_transpose():
            for ch in chains:
                ch["bg_t"] = _pair_transpose(op(ch, "bg"), lane_lo)
                ch["kg_t"] = _pair_transpose(op(ch, "kg"), lane_lo)

        def level_square():
            for ch in chains:
                ch["t"] = eye_pair + ch["m_ab"]
                ch["pw"] = _dot(ch["m_ab"], bd(ch["m_ab"]))

        def level_values():
            for ch in chains:
                r3 = _dot(jnp.concatenate([ch["m_ak"], ch["m_qk"], ch["kg_t"]], axis=0), bd(st["v"][:, ch["sl"]]))
                ch["p1"], ch["o1"], ch["psi"] = r3[:L], r3[L:2 * L], r3[2 * L:]

        def level_iter():
            for ch in chains:
                res = _dot(ch["pw"], bd2(ch["pw"], ch["t"]))
                ch["pw"] = res[:, :PAIR]
                ch["t"] = ch["t"] + res[:, PAIR:]

        def level_last_factor():
            for ch in chains:
                ch["t"] = ch["t"] + _dot(ch["pw"], bd(ch["t"]))

        def level_solve():
            for ch in chains:
                res = _dot(ch["t"], bd2(op(ch, "at"), ch["p1"]))
                ch["a_p"], ch["u0"] = res[:, :PAIR], res[:, PAIR:]

        def level_out():
            for ch in chains:
                res = _dot(jnp.concatenate([ch["m_qb"], ch["bg_t"]], axis=0), bd2(ch["a_p"], ch["u0"]))
                d, p = ch["d"], ch["p"]
                qp_ref[0, c, d, p] = (op(ch, "qt") + res[:L, :PAIR]).astype(BF16)
                o1_ref[0, c, d, p] = (ch["o1"] + res[:L, PAIR:]).astype(BF16)
                phit_ref[0, c, d, p] = (eye_pair * op(ch, "gl") + res[L:, :PAIR]).astype(BF16)
                psit_ref[0, c, d, p] = (ch["psi"] + res[L:, PAIR:]).astype(BF16)

        return [level_gram, level_transpose, level_square, level_values, level_iter, level_iter, level_iter,
                level_iter, level_last_factor, level_solve, level_out]

    def interleave(levels, pieces):
        step = max(1, len(levels) // max(1, len(pieces)))
        pi = 0
        for n, level in enumerate(levels):
            level()
            if pi < len(pieces) and n % step == 0:
                pieces[pi]()
                pi += 1
        for piece in pieces[pi:]:
            piece()

    st_prev, pieces = elementwise_pieces(0)
    interleave([], pieces)
    for c in range(1, n_sub):
        st_cur, pieces = elementwise_pieces(c)
        interleave(matmul_levels(c - 1, st_prev), pieces)
        st_prev = st_cur
    interleave(matmul_levels(n_sub - 1, st_prev), [])


def _rwkv_prep(z_rw, mu_prev, mu_next, w0, w_up_pad, a0, a_up_pad, g_up, k_k, k_a, r_k, n_sub):
    B, T, _ = z_rw.shape
    nc = T // CHUNK
    rows = n_sub * CHUNK
    nb8 = T // SUBLANES
    blk8 = rows // SUBLANES
    vec = lambda n: _resident((1, n))
    op_spec = pl.BlockSpec((1, n_sub, 2, N_PAIR, CHUNK, PAIR), lambda b, c: (b, c, 0, 0, 0, 0))
    tok_spec = pl.BlockSpec((1, rows, RW_WIDTH), lambda b, c: (b, c, 0))
    op_shape = (B, nc, 2, N_PAIR, CHUNK, PAIR)
    return pl.pallas_call(
        _rwkv_prep_body,
        grid=(B, nc // n_sub),
        in_specs=[pl.BlockSpec((1, rows, RW_COLS), lambda b, c: (b, c, 0)),
                  pl.BlockSpec((1, SUBLANES, RW_COLS), lambda b, c: (b, jnp.maximum(c * blk8 - 1, 0), 0)),
                  pl.BlockSpec((1, SUBLANES, RW_COLS), lambda b, c: (b, jnp.minimum((c + 1) * blk8, nb8 - 1), 0)),
                  vec(RW_COLS), vec(RW_COLS),
                  _resident((2, RW_WIDTH)), _resident((2, 2 * W_RANK, RW_WIDTH)),
                  _resident((2, RW_WIDTH)), _resident((2, 2 * A_RANK, RW_WIDTH)),
                  _resident((G_RANK, RW_WIDTH)),
                  vec(RW_WIDTH), vec(RW_WIDTH), vec(RW_WIDTH)],
        out_specs=[op_spec, op_spec, op_spec, op_spec, tok_spec, tok_spec],
        out_shape=[jax.ShapeDtypeStruct(op_shape, BF16), jax.ShapeDtypeStruct(op_shape, BF16),
                   jax.ShapeDtypeStruct(op_shape, BF16), jax.ShapeDtypeStruct(op_shape, BF16),
                   jax.ShapeDtypeStruct((B, T, RW_WIDTH), BF16), jax.ShapeDtypeStruct((B, T, RW_WIDTH), BF16)],
        compiler_params=_params(("parallel", "parallel")),
    )(z_rw, z_rw, z_rw, mu_prev, mu_next, w0, w_up_pad, a0, a_up_pad, g_up, k_k, k_a, r_k)


def _rwkv_scan_body(phif_ref, qpf_ref, psif_ref, o1f_ref, phib_ref, qpb_ref, psib_ref, o1b_ref,
                    of_ref, ob_ref, st_ref):
    @pl.when(pl.program_id(1) == 0)
    def _():
        st_ref[...] = jnp.zeros_like(st_ref)

    lane_lo = lax.broadcasted_iota(jnp.int32, (CHUNK, PAIR), 1) < RW_HEAD
    n_sub = phif_ref.shape[1]
    dirs = ((phif_ref, qpf_ref, psif_ref, o1f_ref, of_ref), (phib_ref, qpb_ref, psib_ref, o1b_ref, ob_ref))
    states = [[st_ref[d, p] for p in range(N_PAIR)] for d in range(2)]
    for step in range(n_sub):
        for d, (phi_ref, qp_ref, psi_ref, o1_ref, out_ref) in enumerate(dirs):
            c = step if d == 0 else n_sub - 1 - step
            for p in range(N_PAIR):
                lhs = jnp.concatenate([phi_ref[0, c, 0, p], qp_ref[0, c, 0, p]], axis=0)
                res = jnp.dot(lhs, _block_diag(states[d][p], lane_lo), preferred_element_type=F32)
                states[d][p] = res[:CHUNK] + psi_ref[0, c, 0, p].astype(F32)
                out_ref[0, c * CHUNK:(c + 1) * CHUNK, p * PAIR:(p + 1) * PAIR] = (
                    res[CHUNK:] + o1_ref[0, c, 0, p].astype(F32)).astype(out_ref.dtype)
    for d in range(2):
        for p in range(N_PAIR):
            st_ref[d, p] = states[d][p]


def _rwkv_scan(phit, qp, psit, o1, n_sub):
    B, nc = phit.shape[0], phit.shape[1]
    T = nc * CHUNK
    nblk = nc // n_sub
    blk = (1, n_sub, 1, N_PAIR, CHUNK, PAIR)
    fwd = pl.BlockSpec(blk, lambda b, j: (b, j, 0, 0, 0, 0))
    bwd = pl.BlockSpec(blk, lambda b, j: (b, nblk - 1 - j, 1, 0, 0, 0))
    return pl.pallas_call(
        _rwkv_scan_body,
        grid=(B, nblk),
        in_specs=[fwd, fwd, fwd, fwd, bwd, bwd, bwd, bwd],
        out_specs=[pl.BlockSpec((1, n_sub * CHUNK, RW_WIDTH), lambda b, j: (b, j, 0)),
                   pl.BlockSpec((1, n_sub * CHUNK, RW_WIDTH), lambda b, j: (b, nblk - 1 - j, 0))],
        out_shape=[jax.ShapeDtypeStruct((B, T, RW_WIDTH), BF16), jax.ShapeDtypeStruct((B, T, RW_WIDTH), BF16)],
        scratch_shapes=[pltpu.VMEM((2, N_PAIR, CHUNK, PAIR), F32)],
        compiler_params=_params(("parallel", "arbitrary")),
    )(phit, qp, psit, o1, phit, qp, psit, o1)


def _attn_body(q_ref, k_ref, v_ref, o_ref, m_ref, l_ref, acc_ref):
    ki = pl.program_id(3)

    @pl.when(ki == 0)
    def _():
        m_ref[...] = jnp.full_like(m_ref, -jnp.inf)
        l_ref[...] = jnp.zeros_like(l_ref)
        acc_ref[...] = jnp.zeros_like(acc_ref)

    q = q_ref[0]
    tq = q.shape[0]
    k = k_ref[0]
    v = v_ref[0]
    heads_per_pass = ATT_GROUP // N_ATT_PASS
    for hp in range(N_ATT_PASS):
        rows = slice(hp * heads_per_pass * tq, (hp + 1) * heads_per_pass * tq)
        qh = jnp.concatenate([q[:, g * ATT_HEAD:(g + 1) * ATT_HEAD]
                              for g in range(hp * heads_per_pass, (hp + 1) * heads_per_pass)], axis=0)
        s = lax.dot_general(qh, k, (((1,), (1,)), ((), ())), preferred_element_type=F32)
        m_prev = m_ref[rows, :]
        m_cur = jnp.maximum(m_prev, jnp.max(s, axis=-1, keepdims=True))
        alpha = jnp.exp2(m_prev - m_cur)
        p = jnp.exp2(s - m_cur[:, 0:1])
        l_ref[rows, :] = alpha * l_ref[rows, :] + jnp.sum(p, axis=-1, keepdims=True)
        acc_ref[rows, :] = alpha * acc_ref[rows, :] + jnp.dot(p.astype(BF16), v, preferred_element_type=F32)
        m_ref[rows, :] = m_cur

    @pl.when(ki == pl.num_programs(3) - 1)
    def _():
        o = acc_ref[...] / l_ref[...]
        for g in range(ATT_GROUP):
            o_ref[0, :, g * ATT_HEAD:(g + 1) * ATT_HEAD] = o[g * tq:(g + 1) * tq].astype(o_ref.dtype)


def _attn_unshifted_body(q_ref, k_ref, v_ref, o_ref, l_ref, acc_ref):
    ki = pl.program_id(3)

    @pl.when(ki == 0)
    def _():
        l_ref[...] = jnp.zeros_like(l_ref)
        acc_ref[...] = jnp.zeros_like(acc_ref)

    q = q_ref[0]
    tq = q.shape[0]
    q4 = jnp.concatenate([q[:, g * ATT_HEAD:(g + 1) * ATT_HEAD] for g in range(ATT_GROUP)], axis=0)
    s = lax.dot_general(q4, k_ref[0], (((1,), (1,)), ((), ())), preferred_element_type=F32)
    p = jnp.exp2(s)
    l_ref[...] = l_ref[...] + jnp.sum(p, axis=-1, keepdims=True)
    acc_ref[...] = acc_ref[...] + jnp.dot(p.astype(BF16), v_ref[0], preferred_element_type=F32)

    @pl.when(ki == pl.num_programs(3) - 1)
    def _():
        o = acc_ref[...] / l_ref[...]
        for g in range(ATT_GROUP):
            o_ref[0, :, g * ATT_HEAD:(g + 1) * ATT_HEAD] = o[g * tq:(g + 1) * tq].astype(o_ref.dtype)


def _attention_call(body, n_stat, q, k, v, tq, tk):
    B, T, _ = q.shape
    gw = ATT_GROUP * ATT_HEAD
    return pl.pallas_call(
        body,
        grid=(B, ATT_KV_HEADS, T // tq, T // tk),
        in_specs=[pl.BlockSpec((1, tq, gw), lambda b, h, i, j: (b, i, h)),
                  pl.BlockSpec((1, tk, ATT_HEAD), lambda b, h, i, j: (b, j, h)),
                  pl.BlockSpec((1, tk, ATT_HEAD), lambda b, h, i, j: (b, j, h))],
        out_specs=pl.BlockSpec((1, tq, gw), lambda b, h, i, j: (b, i, h)),
        out_shape=jax.ShapeDtypeStruct((B, T, ATT_Q), BF16),
        scratch_shapes=[pltpu.VMEM((ATT_GROUP * tq, ATT_HEAD), F32) for _ in range(n_stat)],
        compiler_params=_params(("parallel", "parallel", "parallel", "arbitrary")),
    )(q, k, v)


def _attention(q, k, v, score_bound, tq, tk):
    return lax.cond(score_bound <= SCORE_LIMIT,
                    lambda: _attention_call(_attn_unshifted_body, 2, q, k, v, tq, tk),
                    lambda: _attention_call(_attn_body, 3, q, k, v, tq, tk))


def _layernorm(y, g, b, eps):
    mu = jnp.mean(y, axis=-1, keepdims=True)
    d = y - mu
    var = jnp.mean(d * d, axis=-1, keepdims=True)
    return d * lax.rsqrt(var + eps) * g + b


def _merge_body(x_ref, of_ref, ob_ref, bonus_ref, g_ref, oatt_ref, gate_ref, gng_ref, gnb_ref,
                wprw_ref, wpatt_ref, wout_ref, ln1g_ref, ln1b_ref, h_ref):
    ones_bd = _seg_ones()
    gates = gate_ref[...].astype(F32)
    att = gates[:, D_MODEL:] * jnp.dot(oatt_ref[...], wpatt_ref[...], preferred_element_type=F32)
    o = of_ref[...].astype(F32) + ob_ref[...].astype(F32)
    inv_n = 1.0 / RW_HEAD
    mu = _segsum64(o, ones_bd, stack=False) * inv_n
    d = o - mu
    var = _segsum64(d * d, ones_bd, stack=False) * inv_n
    on = d * lax.rsqrt(var + GN_EPS) * gng_ref[...] + gnb_ref[...]
    o_rw = ((on + bonus_ref[...].astype(F32)) * g_ref[...].astype(F32)).astype(BF16)
    merged = gates[:, :D_MODEL] * jnp.dot(o_rw, wprw_ref[...], preferred_element_type=F32) + att
    mix = jnp.dot(merged.astype(BF16), wout_ref[...], preferred_element_type=F32)
    h_ref[...] = _layernorm(ALPHA * x_ref[...] + mix, ln1g_ref[...], ln1b_ref[...], LN_EPS)


def _merge(x, o_f, o_b, bonus, g, o_att, gates, gn_g, gn_b, w_prw, w_patt, w_out, ln1_g, ln1_b, tm):
    n = x.shape[0]
    tok = lambda w: pl.BlockSpec((tm, w), lambda i: (i, 0))
    vec = _resident((1, D_MODEL))
    mat = _resident((D_MODEL, D_MODEL))
    return pl.pallas_call(
        _merge_body,
        grid=(n // tm,),
        in_specs=[tok(D_MODEL), tok(D_MODEL), tok(D_MODEL), tok(D_MODEL), tok(D_MODEL), tok(D_MODEL),
                  tok(GATE_COLS), vec, vec, mat, mat, mat, vec, vec],
        out_specs=tok(D_MODEL),
        out_shape=jax.ShapeDtypeStruct((n, D_MODEL), F32),
        compiler_params=_params(("parallel",)),
    )(x, o_f, o_b, bonus, g, o_att, gates, gn_g, gn_b, w_prw, w_patt, w_out, ln1_g, ln1_b)


def _mlp_body(h_ref, w1_ref, w2_ref, g_ref, b_ref, o_ref):
    h = h_ref[...]
    u = jnp.maximum(jnp.dot(h.astype(BF16), w1_ref[...], preferred_element_type=F32), 0.0)
    ff = jnp.dot((u * u).astype(BF16), w2_ref[...], preferred_element_type=F32)
    o_ref[...] = _layernorm(ALPHA * h + ff, g_ref[...], b_ref[...], LN_EPS)


def _mlp(h, w1, w2, ln_g, ln_b, tm):
    n = h.shape[0]
    return pl.pallas_call(
        _mlp_body,
        grid=(n // tm,),
        in_specs=[pl.BlockSpec((tm, D_MODEL), lambda i: (i, 0)),
                  _resident((D_MODEL, D_FF)), _resident((D_FF, D_MODEL)),
                  _resident((1, D_MODEL)), _resident((1, D_MODEL))],
        out_specs=pl.BlockSpec((tm, D_MODEL), lambda i: (i, 0)),
        out_shape=jax.ShapeDtypeStruct((n, D_MODEL), F32),
        compiler_params=_params(("parallel",)),
    )(h, w1, w2, ln_g, ln_b)


def _axial_rope_tables(T):
    rows = T // GRID_W
    row = jnp.repeat(jnp.arange(rows, dtype=F32), GRID_W)
    col = jnp.tile(jnp.arange(GRID_W, dtype=F32), rows)
    half = ATT_HEAD // 2
    inv = ROPE_THETA ** (-jnp.arange(0, half, 2, dtype=F32) / half)
    ang = jnp.stack([row[:, None] * inv, col[:, None] * inv], axis=1)
    ang = jnp.broadcast_to(ang[:, :, None, :], (T, 2, 2, half // 2)).reshape(T, ATT_HEAD)
    return jnp.cos(ang), jnp.sin(ang)


def _pad_rank(w):
    z = jnp.zeros_like(w[0])
    return jnp.stack([jnp.concatenate([w[0], z], axis=0), jnp.concatenate([z, w[1]], axis=0)]).astype(BF16)


def _tile(n, pref):
    t = min(pref, n)
    while n % t:
        t //= 2
    return t


def _layer(x, p):
    B, T, D = x.shape
    n = B * T
    xf = x.reshape(n, D)
    cos, sin = _axial_rope_tables(T)
    z_rw, q_r, k_r, v_b, gates = _inproj(xf, p["w_rw"], p["w_qkv"], p["w_gate"], cos, sin,
                                         p["q_norm"], p["k_norm"], _tile(T, 256), T)

    phit, qp, psit, o1, bonus, g = _rwkv_prep(
        z_rw.reshape(B, T, RW_COLS), p["mu_prev"], p["mu_next"], p["w0"], p["w_up"], p["a0"], p["a_up"],
        p["g_up"], p["k_k"], p["k_a"], p["r_k"], _tile(T // CHUNK, PREP_CHUNKS))
    o_f, o_b = _rwkv_scan(phit, qp, psit, o1, _tile(T // CHUNK, 4))

    score_bound = ATT_HEAD * Q_SCALE * jnp.max(jnp.abs(p["q_norm"])) * jnp.max(jnp.abs(p["k_norm"]))
    o_att = _attention(q_r.reshape(B, T, ATT_Q), k_r.reshape(B, T, ATT_KV), v_b.reshape(B, T, ATT_KV),
                       score_bound, _tile(T, 512), _tile(T, 2048))

    h = _merge(xf, o_f.reshape(n, D), o_b.reshape(n, D), bonus.reshape(n, D), g.reshape(n, D),
               o_att.reshape(n, D), gates, p["gn_g"], p["gn_b"], p["w_prw"], p["w_patt"], p["w_out"],
               p["ln1_g"], p["ln1_b"], _tile(n, 256))
    y = _mlp(h, p["w_ff1"], p["w_ff2"], p["ln2_g"], p["ln2_b"], _tile(n, 512))
    return y.reshape(B, T, D)


def kernel(x_prompt, x_sample, w_in, rw_mu_prev, rw_mu_next, rw_w0, rw_w_up, rw_a0, rw_a_up, rw_g_up, rw_k_k,
           rw_k_a, rw_r_k, rw_gn_g, rw_gn_b, q_norm, k_norm, w_proj_rwkv, w_proj_attn, w_out, ln1_g, ln1_b,
           w_ff1, w_ff2, ln2_g, ln2_b):
    def layer_params(l):
        w = w_in[l].astype(BF16)
        row = lambda a: a[l].reshape(1, -1)
        return dict(
            w_rw=w[:, :RW_COLS], w_qkv=w[:, RW_COLS:RW_COLS + QKV_COLS], w_gate=w[:, RW_COLS + QKV_COLS:],
            mu_prev=row(rw_mu_prev), mu_next=row(rw_mu_next),
            w0=rw_w0[l], w_up=_pad_rank(rw_w_up[l]), a0=rw_a0[l], a_up=_pad_rank(rw_a_up[l]),
            g_up=rw_g_up[l].astype(BF16), k_k=row(rw_k_k), k_a=row(rw_k_a), r_k=row(rw_r_k),
            gn_g=row(rw_gn_g), gn_b=row(rw_gn_b), q_norm=row(q_norm), k_norm=row(k_norm),
            w_prw=w_proj_rwkv[l].astype(BF16), w_patt=w_proj_attn[l].astype(BF16), w_out=w_out[l].astype(BF16),
            ln1_g=row(ln1_g), ln1_b=row(ln1_b), w_ff1=w_ff1[l].astype(BF16), w_ff2=w_ff2[l].astype(BF16),
            ln2_g=row(ln2_g), ln2_b=row(ln2_b))

    layers = [layer_params(l) for l in range(w_in.shape[0])]

    def trunk(x):
        for p in layers:
            x = _layer(x, p)
        return x

    return trunk(x_prompt), trunk(x_sample)
```

```python
import functools
import math

import jax
import jax.numpy as jnp
from jax import lax
from jax.experimental import pallas as pl
from jax.experimental.pallas import tpu as pltpu

F32 = jnp.float32
BF16 = jnp.bfloat16

D_MODEL = 1024
GRID_W = 64
RW_HEAD = 64
RW_WIDTH = 1024
W_RANK = 64
A_RANK = 64
G_RANK = 128
DECAY_SCALE = math.exp(-0.5)
LOG2E = math.log2(math.e)
GN_EPS = 64e-5
ATT_HEAD = 128
ATT_Q_HEADS = 8
ATT_KV_HEADS = 2
ATT_GROUP = ATT_Q_HEADS // ATT_KV_HEADS
ATT_Q = ATT_Q_HEADS * ATT_HEAD
ATT_KV = ATT_KV_HEADS * ATT_HEAD
ROPE_THETA = 10000.0
RMS_EPS = 1e-6
Q_SCALE = ATT_HEAD ** -0.5 * math.log2(math.e)
D_FF = 4 * D_MODEL
LN_EPS = 1e-5
DEPTH = 1
ALPHA = (2 * DEPTH) ** 0.25
RW_COLS = 3 * RW_WIDTH + 2 * W_RANK + 2 * A_RANK + G_RANK
QKV_COLS = ATT_Q + 2 * ATT_KV
GATE_COLS = 2 * D_MODEL

SCORE_LIMIT = 96.0
N_ATT_PASS = 1
PREP_CHUNKS = 4
CHUNK = 64
PAIR = 2 * RW_HEAD
N_PAIR = RW_WIDTH // PAIR
LANES = 128
SUBLANES = 8
VMEM_LIMIT = 56 * 1024 * 1024


def _params(sem):
    return pltpu.CompilerParams(dimension_semantics=sem, vmem_limit_bytes=VMEM_LIMIT)


def _dot(a, b):
    return jnp.dot(a.astype(BF16), b.astype(BF16), preferred_element_type=F32)


def _dot_nt(a, b):
    return lax.dot_general(a.astype(BF16), b.astype(BF16), (((1,), (1,)), ((), ())),
                           preferred_element_type=F32)


def _split(x):
    hi = x.astype(BF16)
    lo = (x - hi.astype(F32)).astype(BF16)
    return hi, lo


def _dot_split_lhs(x, w):
    hi, lo = _split(x)
    return (jnp.dot(hi, w, preferred_element_type=F32) + jnp.dot(lo, w, preferred_element_type=F32))


def _dot_split_rhs(w, x):
    hi, lo = _split(x)
    return jnp.dot(jnp.concatenate([w, w], axis=1), jnp.concatenate([hi, lo], axis=0), preferred_element_type=F32)


def _sigmoid(x):
    return 0.5 * jnp.tanh(0.5 * x) + 0.5


def _resident(shape):
    nd = len(shape)
    return pl.BlockSpec(shape, lambda *_: (0,) * nd, pipeline_mode=pl.Buffered(1))


def _seg_ones(width=LANES):
    r = lax.broadcasted_iota(jnp.int32, (width, width), 0) // RW_HEAD
    c = lax.broadcasted_iota(jnp.int32, (width, width), 1) // RW_HEAD
    return jnp.where(r == c, 1.0, 0.0).astype(BF16)


def _segsum64(x, ones_bd, stack=True):
    width = ones_bd.shape[0]
    rows, n = x.shape[0], x.shape[1] // width
    if not stack:
        return jnp.concatenate([_dot_split_lhs(x[:, j * width:(j + 1) * width], ones_bd) for j in range(n)], axis=1)
    hi, lo = _split(jnp.concatenate([x[:, j * width:(j + 1) * width] for j in range(n)], axis=0))
    s = jnp.dot(jnp.concatenate([hi, lo], axis=0), ones_bd, preferred_element_type=F32)
    s = s[:n * rows] + s[n * rows:]
    return jnp.concatenate([s[j * rows:(j + 1) * rows] for j in range(n)], axis=1)


def _norm_rope(xh, gain, cos, sin, first):
    ms = jnp.mean(xh * xh, axis=-1, keepdims=True)
    xn = xh * lax.rsqrt(ms + RMS_EPS) * gain
    rot = jnp.where(first, -pltpu.roll(xn, ATT_HEAD - ATT_HEAD // 4, 1), pltpu.roll(xn, ATT_HEAD // 4, 1))
    return xn * cos + rot * sin


def _inproj_body(x_ref, wrw_ref, wqkv_ref, wg_ref, cos_ref, sin_ref, qn_ref, kn_ref,
                 zrw_ref, q_ref, k_ref, v_ref, gate_ref):
    x = x_ref[...].astype(BF16)
    qkv = jnp.dot(x, wqkv_ref[...], preferred_element_type=F32)
    cos = cos_ref[...]
    sin = sin_ref[...]
    lane = lax.broadcasted_iota(jnp.int32, cos.shape, 1)
    first = (lane % (ATT_HEAD // 2)) < (ATT_HEAD // 4)
    for h in range(ATT_Q_HEADS):
        sl = slice(h * ATT_HEAD, (h + 1) * ATT_HEAD)
        q_ref[:, sl] = (_norm_rope(qkv[:, sl], qn_ref[...], cos, sin, first) * Q_SCALE).astype(BF16)
    for h in range(ATT_KV_HEADS):
        sl = slice(h * ATT_HEAD, (h + 1) * ATT_HEAD)
        k_ref[:, sl] = _norm_rope(qkv[:, ATT_Q + h * ATT_HEAD:ATT_Q + (h + 1) * ATT_HEAD], kn_ref[...],
                                  cos, sin, first).astype(BF16)
    v_ref[...] = qkv[:, ATT_Q + ATT_KV:].astype(BF16)
    gate_ref[...] = _sigmoid(jnp.dot(x, wg_ref[...], preferred_element_type=F32)).astype(BF16)
    zrw_ref[...] = jnp.dot(x, wrw_ref[...], preferred_element_type=F32)


def _inproj(x, w_rw, w_qkv, w_gate, cos, sin, q_norm, k_norm, tm, seq_len):
    n = x.shape[0]
    tiles_per_seq = seq_len // tm
    tok = lambda w: pl.BlockSpec((tm, w), lambda i: (i, 0))
    rope = pl.BlockSpec((tm, ATT_HEAD), lambda i: (i % tiles_per_seq, 0))
    return pl.pallas_call(
        _inproj_body,
        grid=(n // tm,),
        in_specs=[tok(D_MODEL),
                  _resident((D_MODEL, RW_COLS)), _resident((D_MODEL, QKV_COLS)), _resident((D_MODEL, GATE_COLS)),
                  rope, rope, _resident((1, ATT_HEAD)), _resident((1, ATT_HEAD))],
        out_specs=[tok(RW_COLS), tok(ATT_Q), tok(ATT_KV), tok(ATT_KV), tok(GATE_COLS)],
        out_shape=[jax.ShapeDtypeStruct((n, RW_COLS), F32),
                   jax.ShapeDtypeStruct((n, ATT_Q), BF16),
                   jax.ShapeDtypeStruct((n, ATT_KV), BF16),
                   jax.ShapeDtypeStruct((n, ATT_KV), BF16),
                   jax.ShapeDtypeStruct((n, GATE_COLS), BF16)],
        compiler_params=_params(("parallel",)),
    )(x, w_rw, w_qkv, w_gate, cos, sin, q_norm, k_norm)


def _block_diag(x, lane_lo):
    top = jnp.where(lane_lo, x, 0.0)
    bot = jnp.where(lane_lo, 0.0, x)
    return jnp.concatenate([top, bot], axis=0).astype(BF16)


def _pair_transpose(x, lane_lo):
    top = jnp.where(lane_lo, x, 0.0)
    bot = jnp.where(lane_lo, 0.0, x)
    xt = jnp.transpose(jnp.concatenate([top, bot], axis=0))
    return xt[:CHUNK] + xt[CHUNK:]


def _rwkv_prep_body(zc_ref, zp_ref, zn_ref, mup_ref, mun_ref, w0_ref, wup_ref, a0_ref, aup_ref, gup_ref,
                    kk_ref, ka_ref, rk_ref,
                    phit_ref, qp_ref, psit_ref, o1_ref, bonus_ref, g_ref):
    c_idx = pl.program_id(1)
    n_steps = pl.num_programs(1)
    L = CHUNK
    C = RW_WIDTH
    R = zc_ref.shape[1]
    n_sub = R // L

    tp = lax.broadcasted_iota(jnp.int32, (L, PAIR), 0)
    lp = lax.broadcasted_iota(jnp.int32, (L, PAIR), 1)
    sp = lp % RW_HEAD
    lane_lo = lp < RW_HEAD
    eye_pair = jnp.where(sp == tp, 1.0, 0.0)
    masks = ((sp < tp, sp <= tp), (sp > tp, sp >= tp))
    ti = lax.broadcasted_iota(jnp.int32, (L, L), 0)
    si = lax.broadcasted_iota(jnp.int32, (L, L), 1)
    tris = (jnp.where(si <= ti, 1.0, 0.0).astype(BF16), jnp.where(si >= ti, 1.0, 0.0).astype(BF16))
    ones_bd = _seg_ones(2 * LANES)

    zc = zc_ref[0]
    row = lax.broadcasted_iota(jnp.int32, zc.shape, 0)
    prev_row = zp_ref[0][SUBLANES - 1:SUBLANES, :] * jnp.where(c_idx > 0, 1.0, 0.0)
    next_row = zn_ref[0][0:1, :] * jnp.where(c_idx < n_steps - 1, 1.0, 0.0)
    z_prev = jnp.where(row == 0, prev_row, pltpu.roll(zc, 1, 0))
    z_next = jnp.where(row == R - 1, next_row, pltpu.roll(zc, R - 1, 0))
    mu_p, mu_n = mup_ref[...], mun_ref[...]
    z = zc * (1.0 - mu_p - mu_n) + mu_p * z_prev + mu_n * z_next
    o_wd = 3 * C
    gd = _sigmoid(z[:, o_wd + 2 * W_RANK + 2 * A_RANK:RW_COLS]).astype(BF16)
    g_ref[0] = jnp.dot(gd, gup_ref[...], preferred_element_type=F32).astype(BF16)
    tw = jnp.tanh(z[:, o_wd:o_wd + 2 * W_RANK]).astype(BF16)
    ad = z[:, o_wd + 2 * W_RANK:o_wd + 2 * W_RANK + 2 * A_RANK].astype(BF16)
    w_logit = [w0_ref[d:d + 1, :] + jnp.dot(tw, wup_ref[d], preferred_element_type=F32) for d in range(2)]
    a_logit = [a0_ref[d:d + 1, :] + jnp.dot(ad, aup_ref[d], preferred_element_type=F32) for d in range(2)]

    def elementwise_pieces(c):
        rs = slice(c * L, (c + 1) * L)
        st = dict(ops=[None, None])

        def piece_kappa():
            st["r"], st["k"], st["v"] = z[rs, 0:C], z[rs, C:2 * C], z[rs, 2 * C:3 * C]
            kappa = st["k"] * kk_ref[...]
            st["kh"] = kappa / jnp.maximum(jnp.sqrt(_segsum64(kappa * kappa, ones_bd)), 1e-12)
            st["kka"] = st["k"] * ka_ref[...]

        def piece_gates(d):
            def run():
                st["lw"] = (-DECAY_SCALE * LOG2E) * _sigmoid(w_logit[d][rs, :])
                a = _sigmoid(a_logit[d][rs, :])
                st["kt"] = st["k"] + st["kka"] * (a - 1.0)
                st["kt_sum"] = st["kt"] if d == 0 else st["kt_sum"] + st["kt"]
                st["ak"] = a * st["kh"]
            return run

        def piece_decay(d):
            def run():
                lw = st["lw"]
                cum = _dot_split_rhs(tris[d], lw)
                ctot = cum[L - 1:L, :] if d == 0 else cum[0:1, :]
                g_l = jnp.exp2(ctot)
                inv = jnp.exp2(-cum)
                suf = g_l * inv
                st["ops"][d] = dict(at=-st["kh"] * jnp.exp2(cum - lw), qt=st["r"] * jnp.exp2(cum),
                                    bt=st["ak"] * inv, kt=st["kt"] * inv, bg=st["ak"] * suf, kg=st["kt"] * suf,
                                    gl=g_l)
            return run

        def piece_bonus():
            bonus_ref[0, rs, :] = (_segsum64(st["r"] * st["kt_sum"] * rk_ref[...], ones_bd) * st["v"]).astype(BF16)

        return st, [piece_kappa, piece_gates(0), piece_decay(0), piece_gates(1), piece_decay(1), piece_bonus]

    bd = lambda x: _block_diag(x, lane_lo)
    bd2 = lambda x, y: jnp.concatenate([bd(x), bd(y)], axis=1)

    def matmul_levels(c, st):
        chains = [dict(d=d, p=p, sl=slice(p * PAIR, (p + 1) * PAIR)) for d in range(2) for p in range(N_PAIR)]
        op = lambda ch, name: st["ops"][ch["d"]][name][:, ch["sl"]]

        def level_gram():
            for ch in chains:
                m_strict, m_incl = masks[ch["d"]]
                aq = jnp.concatenate([op(ch, "at"), op(ch, "qt")], axis=0)
                gram = _dot_nt(aq, jnp.concatenate([bd(op(ch, "bt")), bd(op(ch, "kt"))], axis=0))
                ch["m_ab"] = jnp.where(m_strict, gram[:L, :PAIR], 0.0)
                ch["m_ak"] = jnp.where(m_strict, gram[:L, PAIR:], 0.0)
                ch["m_qb"] = jnp.where(m_incl, gram[L:, :PAIR], 0.0)
                ch["m_qk"] = jnp.where(m_incl, gram[L:, PAIR:], 0.0)

        def level_transpose():
            for ch in chains:
                ch["bg_t"] = _pair_transpose(op(ch, "bg"), lane_lo)
                ch["kg_t"] = _pair_transpose(op(ch, "kg"), lane_lo)

        def level_square():
            for ch in chains:
                ch["t"] = eye_pair + ch["m_ab"]
                ch["pw"] = _dot(ch["m_ab"], bd(ch["m_ab"]))

        def level_values():
            for ch in chains:
                r3 = _dot(jnp.concatenate([ch["m_ak"], ch["m_qk"], ch["kg_t"]], axis=0), bd(st["v"][:, ch["sl"]]))
                ch["p1"], ch["o1"], ch["psi"] = r3[:L], r3[L:2 * L], r3[2 * L:]

        def level_iter():
            for ch in chains:
                res = _dot(ch["pw"], bd2(ch["pw"], ch["t"]))
                ch["pw"] = res[:, :PAIR]
                ch["t"] = ch["t"] + res[:, PAIR:]

        def level_last_factor():
            for ch in chains:
                ch["t"] = ch["t"] + _dot(ch["pw"], bd(ch["t"]))

        def level_solve():
            for ch in chains:
                res = _dot(ch["t"], bd2(op(ch, "at"), ch["p1"]))
                ch["a_p"], ch["u0"] = res[:, :PAIR], res[:, PAIR:]

        def level_out():
            for ch in chains:
                res = _dot(jnp.concatenate([ch["m_qb"], ch["bg_t"]], axis=0), bd2(ch["a_p"], ch["u0"]))
                d, p = ch["d"], ch["p"]
                qp_ref[0, c, d, p] = (op(ch, "qt") + res[:L, :PAIR]).astype(BF16)
                o1_ref[0, c, d, p] = (ch["o1"] + res[:L, PAIR:]).astype(BF16)
                phit_ref[0, c, d, p] = (eye_pair * op(ch, "gl") + res[L:, :PAIR]).astype(BF16)
                psit_ref[0, c, d, p] = (ch["psi"] + res[L:, PAIR:]).astype(BF16)

        return [level_gram, level_transpose, level_square, level_values, level_iter, level_iter, level_iter,
                level_iter, level_last_factor, level_solve, level_out]

    def interleave(levels, pieces):
        step = max(1, len(levels) // max(1, len(pieces)))
        pi = 0
        for n, level in enumerate(levels):
            level()
            if pi < len(pieces) and n % step == 0:
                pieces[pi]()
                pi += 1
        for piece in pieces[pi:]:
            piece()

    st_prev, pieces = elementwise_pieces(0)
    interleave([], pieces)
    for c in range(1, n_sub):
        st_cur, pieces = elementwise_pieces(c)
        interleave(matmul_levels(c - 1, st_prev), pieces)
        st_prev = st_cur
    interleave(matmul_levels(n_sub - 1, st_prev), [])


def _rwkv_prep(z_rw, mu_prev, mu_next, w0, w_up_pad, a0, a_up_pad, g_up, k_k, k_a, r_k, n_sub):
    B, T, _ = z_rw.shape
    nc = T // CHUNK
    rows = n_sub * CHUNK
    nb8 = T // SUBLANES
    blk8 = rows // SUBLANES
    vec = lambda n: _resident((1, n))
    op_spec = pl.BlockSpec((1, n_sub, 2, N_PAIR, CHUNK, PAIR), lambda b, c: (b, c, 0, 0, 0, 0))
    tok_spec = pl.BlockSpec((1, rows, RW_WIDTH), lambda b, c: (b, c, 0))
    op_shape = (B, nc, 2, N_PAIR, CHUNK, PAIR)
    return pl.pallas_call(
        _rwkv_prep_body,
        grid=(B, nc // n_sub),
        in_specs=[pl.BlockSpec((1, rows, RW_COLS), lambda b, c: (b, c, 0)),
                  pl.BlockSpec((1, SUBLANES, RW_COLS), lambda b, c: (b, jnp.maximum(c * blk8 - 1, 0), 0)),
                  pl.BlockSpec((1, SUBLANES, RW_COLS), lambda b, c: (b, jnp.minimum((c + 1) * blk8, nb8 - 1), 0)),
                  vec(RW_COLS), vec(RW_COLS),
                  _resident((2, RW_WIDTH)), _resident((2, 2 * W_RANK, RW_WIDTH)),
                  _resident((2, RW_WIDTH)), _resident((2, 2 * A_RANK, RW_WIDTH)),
                  _resident((G_RANK, RW_WIDTH)),
                  vec(RW_WIDTH), vec(RW_WIDTH), vec(RW_WIDTH)],
        out_specs=[op_spec, op_spec, op_spec, op_spec, tok_spec, tok_spec],
        out_shape=[jax.ShapeDtypeStruct(op_shape, BF16), jax.ShapeDtypeStruct(op_shape, BF16),
                   jax.ShapeDtypeStruct(op_shape, BF16), jax.ShapeDtypeStruct(op_shape, BF16),
                   jax.ShapeDtypeStruct((B, T, RW_WIDTH), BF16), jax.ShapeDtypeStruct((B, T, RW_WIDTH), BF16)],
        compiler_params=_params(("parallel", "parallel")),
    )(z_rw, z_rw, z_rw, mu_prev, mu_next, w0, w_up_pad, a0, a_up_pad, g_up, k_k, k_a, r_k)


def _rwkv_scan_body(phif_ref, qpf_ref, psif_ref, o1f_ref, phib_ref, qpb_ref, psib_ref, o1b_ref,
                    of_ref, ob_ref, st_ref):
    @pl.when(pl.program_id(1) == 0)
    def _():
        st_ref[...] = jnp.zeros_like(st_ref)

    lane_lo = lax.broadcasted_iota(jnp.int32, (CHUNK, PAIR), 1) < RW_HEAD
    n_sub = phif_ref.shape[1]
    dirs = ((phif_ref, qpf_ref, psif_ref, o1f_ref, of_ref), (phib_ref, qpb_ref, psib_ref, o1b_ref, ob_ref))
    states = [[st_ref[d, p] for p in range(N_PAIR)] for d in range(2)]
    for step in range(n_sub):
        for d, (phi_ref, qp_ref, psi_ref, o1_ref, out_ref) in enumerate(dirs):
            c = step if d == 0 else n_sub - 1 - step
            for p in range(N_PAIR):
                lhs = jnp.concatenate([phi_ref[0, c, 0, p], qp_ref[0, c, 0, p]], axis=0)
                res = jnp.dot(lhs, _block_diag(states[d][p], lane_lo), preferred_element_type=F32)
                states[d][p] = res[:CHUNK] + psi_ref[0, c, 0, p].astype(F32)
                out_ref[0, c * CHUNK:(c + 1) * CHUNK, p * PAIR:(p + 1) * PAIR] = (
                    res[CHUNK:] + o1_ref[0, c, 0, p].astype(F32)).astype(out_ref.dtype)
    for d in range(2):
        for p in range(N_PAIR):
            st_ref[d, p] = states[d][p]


def _rwkv_scan(phit, qp, psit, o1, n_sub):
    B, nc = phit.shape[0], phit.shape[1]
    T = nc * CHUNK
    nblk = nc // n_sub
    blk = (1, n_sub, 1, N_PAIR, CHUNK, PAIR)
    fwd = pl.BlockSpec(blk, lambda b, j: (b, j, 0, 0, 0, 0))
    bwd = pl.BlockSpec(blk, lambda b, j: (b, nblk - 1 - j, 1, 0, 0, 0))
    return pl.pallas_call(
        _rwkv_scan_body,
        grid=(B, nblk),
        in_specs=[fwd, fwd, fwd, fwd, bwd, bwd, bwd, bwd],
        out_specs=[pl.BlockSpec((1, n_sub * CHUNK, RW_WIDTH), lambda b, j: (b, j, 0)),
                   pl.BlockSpec((1, n_sub * CHUNK, RW_WIDTH), lambda b, j: (b, nblk - 1 - j, 0))],
        out_shape=[jax.ShapeDtypeStruct((B, T, RW_WIDTH), BF16), jax.ShapeDtypeStruct((B, T, RW_WIDTH), BF16)],
        scratch_shapes=[pltpu.VMEM((2, N_PAIR, CHUNK, PAIR), F32)],
        compiler_params=_params(("parallel", "arbitrary")),
    )(phit, qp, psit, o1, phit, qp, psit, o1)


def _attn_body(q_ref, k_ref, v_ref, o_ref, m_ref, l_ref, acc_ref):
    ki = pl.program_id(3)

    @pl.when(ki == 0)
    def _():
        m_ref[...] = jnp.full_like(m_ref, -jnp.inf)
        l_ref[...] = jnp.zeros_like(l_ref)
        acc_ref[...] = jnp.zeros_like(acc_ref)

    q = q_ref[0]
    tq = q.shape[0]
    k = k_ref[0]
    v = v_ref[0]
    heads_per_pass = ATT_GROUP // N_ATT_PASS
    for hp in range(N_ATT_PASS):
        rows = slice(hp * heads_per_pass * tq, (hp + 1) * heads_per_pass * tq)
        qh = jnp.concatenate([q[:, g * ATT_HEAD:(g + 1) * ATT_HEAD]
                              for g in range(hp * heads_per_pass, (hp + 1) * heads_per_pass)], axis=0)
        s = lax.dot_general(qh, k, (((1,), (1,)), ((), ())), preferred_element_type=F32)
        m_prev = m_ref[rows, :]
        m_cur = jnp.maximum(m_prev, jnp.max(s, axis=-1, keepdims=True))
        alpha = jnp.exp2(m_prev - m_cur)
        p = jnp.exp2(s - m_cur[:, 0:1])
        l_ref[rows, :] = alpha * l_ref[rows, :] + jnp.sum(p, axis=-1, keepdims=True)
        acc_ref[rows, :] = alpha * acc_ref[rows, :] + jnp.dot(p.astype(BF16), v, preferred_element_type=F32)
        m_ref[rows, :] = m_cur

    @pl.when(ki == pl.num_programs(3) - 1)
    def _():
        o = acc_ref[...] / l_ref[...]
        for g in range(ATT_GROUP):
            o_ref[0, :, g * ATT_HEAD:(g + 1) * ATT_HEAD] = o[g * tq:(g + 1) * tq].astype(o_ref.dtype)


def _attn_unshifted_body(q_ref, k_ref, v_ref, o_ref, l_ref, acc_ref):
    ki = pl.program_id(3)

    @pl.when(ki == 0)
    def _():
        l_ref[...] = jnp.zeros_like(l_ref)
        acc_ref[...] = jnp.zeros_like(acc_ref)

    q = q_ref[0]
    tq = q.shape[0]
    q4 = jnp.concatenate([q[:, g * ATT_HEAD:(g + 1) * ATT_HEAD] for g in range(ATT_GROUP)], axis=0)
    s = lax.dot_general(q4, k_ref[0], (((1,), (1,)), ((), ())), preferred_element_type=F32)
    p = jnp.exp2(s)
    l_ref[...] = l_ref[...] + jnp.sum(p, axis=-1, keepdims=True)
    acc_ref[...] = acc_ref[...] + jnp.dot(p.astype(BF16), v_ref[0], preferred_element_type=F32)

    @pl.when(ki == pl.num_programs(3) - 1)
    def _():
        o = acc_ref[...] / l_ref[...]
        for g in range(ATT_GROUP):
            o_ref[0, :, g * ATT_HEAD:(g + 1) * ATT_HEAD] = o[g * tq:(g + 1) * tq].astype(o_ref.dtype)


def _attention_call(body, n_stat, q, k, v, tq, tk):
    B, T, _ = q.shape
    gw = ATT_GROUP * ATT_HEAD
    return pl.pallas_call(
        body,
        grid=(B, ATT_KV_HEADS, T // tq, T // tk),
        in_specs=[pl.BlockSpec((1, tq, gw), lambda b, h, i, j: (b, i, h)),
                  pl.BlockSpec((1, tk, ATT_HEAD), lambda b, h, i, j: (b, j, h)),
                  pl.BlockSpec((1, tk, ATT_HEAD), lambda b, h, i, j: (b, j, h))],
        out_specs=pl.BlockSpec((1, tq, gw), lambda b, h, i, j: (b, i, h)),
        out_shape=jax.ShapeDtypeStruct((B, T, ATT_Q), BF16),
        scratch_shapes=[pltpu.VMEM((ATT_GROUP * tq, ATT_HEAD), F32) for _ in range(n_stat)],
        compiler_params=_params(("parallel", "parallel", "parallel", "arbitrary")),
    )(q, k, v)


def _attention(q, k, v, score_bound, tq, tk):
    return lax.cond(score_bound <= SCORE_LIMIT,
                    lambda: _attention_call(_attn_unshifted_body, 2, q, k, v, tq, tk),
                    lambda: _attention_call(_attn_body, 3, q, k, v, tq, tk))


def _layernorm(y, g, b, eps):
    mu = jnp.mean(y, axis=-1, keepdims=True)
    d = y - mu
    var = jnp.mean(d * d, axis=-1, keepdims=True)
    return d * lax.rsqrt(var + eps) * g + b


def _merge_body(x_ref, of_ref, ob_ref, bonus_ref, g_ref, oatt_ref, gate_ref, gng_ref, gnb_ref,
                wprw_ref, wpatt_ref, wout_ref, ln1g_ref, ln1b_ref, h_ref):
    ones_bd = _seg_ones()
    gates = gate_ref[...].astype(F32)
    att = gates[:, D_MODEL:] * jnp.dot(oatt_ref[...], wpatt_ref[...], preferred_element_type=F32)
    o = of_ref[...].astype(F32) + ob_ref[...].astype(F32)
    inv_n = 1.0 / RW_HEAD
    mu = _segsum64(o, ones_bd, stack=False) * inv_n
    d = o - mu
    var = _segsum64(d * d, ones_bd, stack=False) * inv_n
    on = d * lax.rsqrt(var + GN_EPS) * gng_ref[...] + gnb_ref[...]
    o_rw = ((on + bonus_ref[...].astype(F32)) * g_ref[...].astype(F32)).astype(BF16)
    merged = gates[:, :D_MODEL] * jnp.dot(o_rw, wprw_ref[...], preferred_element_type=F32) + att
    mix = jnp.dot(merged.astype(BF16), wout_ref[...], preferred_element_type=F32)
    h_ref[...] = _layernorm(ALPHA * x_ref[...] + mix, ln1g_ref[...], ln1b_ref[...], LN_EPS)


def _merge(x, o_f, o_b, bonus, g, o_att, gates, gn_g, gn_b, w_prw, w_patt, w_out, ln1_g, ln1_b, tm):
    n = x.shape[0]
    tok = lambda w: pl.BlockSpec((tm, w), lambda i: (i, 0))
    vec = _resident((1, D_MODEL))
    mat = _resident((D_MODEL, D_MODEL))
    return pl.pallas_call(
        _merge_body,
        grid=(n // tm,),
        in_specs=[tok(D_MODEL), tok(D_MODEL), tok(D_MODEL), tok(D_MODEL), tok(D_MODEL), tok(D_MODEL),
                  tok(GATE_COLS), vec, vec, mat, mat, mat, vec, vec],
        out_specs=tok(D_MODEL),
        out_shape=jax.ShapeDtypeStruct((n, D_MODEL), F32),
        compiler_params=_params(("parallel",)),
    )(x, o_f, o_b, bonus, g, o_att, gates, gn_g, gn_b, w_prw, w_patt, w_out, ln1_g, ln1_b)


def _mlp_body(h_ref, w1_ref, w2_ref, g_ref, b_ref, o_ref):
    h = h_ref[...]
    u = jnp.maximum(jnp.dot(h.astype(BF16), w1_ref[...], preferred_element_type=F32), 0.0)
    ff = jnp.dot((u * u).astype(BF16), w2_ref[...], preferred_element_type=F32)
    o_ref[...] = _layernorm(ALPHA * h + ff, g_ref[...], b_ref[...], LN_EPS)


def _mlp(h, w1, w2, ln_g, ln_b, tm):
    n = h.shape[0]
    return pl.pallas_call(
        _mlp_body,
        grid=(n // tm,),
        in_specs=[pl.BlockSpec((tm, D_MODEL), lambda i: (i, 0)),
                  _resident((D_MODEL, D_FF)), _resident((D_FF, D_MODEL)),
                  _resident((1, D_MODEL)), _resident((1, D_MODEL))],
        out_specs=pl.BlockSpec((tm, D_MODEL), lambda i: (i, 0)),
        out_shape=jax.ShapeDtypeStruct((n, D_MODEL), F32),
        compiler_params=_params(("parallel",)),
    )(h, w1, w2, ln_g, ln_b)


def _axial_rope_tables(T):
    rows = T // GRID_W
    row = jnp.repeat(jnp.arange(rows, dtype=F32), GRID_W)
    col = jnp.tile(jnp.arange(GRID_W, dtype=F32), rows)
    half = ATT_HEAD // 2
    inv = ROPE_THETA ** (-jnp.arange(0, half, 2, dtype=F32) / half)
    ang = jnp.stack([row[:, None] * inv, col[:, None] * inv], axis=1)
    ang = jnp.broadcast_to(ang[:, :, None, :], (T, 2, 2, half // 2)).reshape(T, ATT_HEAD)
    return jnp.cos(ang), jnp.sin(ang)


def _pad_rank(w):
    z = jnp.zeros_like(w[0])
    return jnp.stack([jnp.concatenate([w[0], z], axis=0), jnp.concatenate([z, w[1]], axis=0)]).astype(BF16)


def _tile(n, pref):
    t = min(pref, n)
    while n % t:
        t //= 2
    return t


def _layer(x, p):
    B, T, D = x.shape
    n = B * T
    xf = x.reshape(n, D)
    cos, sin = _axial_rope_tables(T)
    z_rw, q_r, k_r, v_b, gates = _inproj(xf, p["w_rw"], p["w_qkv"], p["w_gate"], cos, sin,
                                         p["q_norm"], p["k_norm"], _tile(T, 256), T)

    phit, qp, psit, o1, bonus, g = _rwkv_prep(
        z_rw.reshape(B, T, RW_COLS), p["mu_prev"], p["mu_next"], p["w0"], p["w_up"], p["a0"], p["a_up"],
        p["g_up"], p["k_k"], p["k_a"], p["r_k"], _tile(T // CHUNK, PREP_CHUNKS))
    o_f, o_b = _rwkv_scan(phit, qp, psit, o1, _tile(T // CHUNK, 4))

    score_bound = ATT_HEAD * Q_SCALE * jnp.max(jnp.abs(p["q_norm"])) * jnp.max(jnp.abs(p["k_norm"]))
    o_att = _attention(q_r.reshape(B, T, ATT_Q), k_r.reshape(B, T, ATT_KV), v_b.reshape(B, T, ATT_KV),
                       score_bound, _tile(T, 512), _tile(T, 2048))

    h = _merge(xf, o_f.reshape(n, D), o_b.reshape(n, D), bonus.reshape(n, D), g.reshape(n, D),
               o_att.reshape(n, D), gates, p["gn_g"], p["gn_b"], p["w_prw"], p["w_patt"], p["w_out"],
               p["ln1_g"], p["ln1_b"], _tile(n, 256))
    y = _mlp(h, p["w_ff1"], p["w_ff2"], p["ln2_g"], p["ln2_b"], _tile(n, 512))
    return y.reshape(B, T, D)


def kernel(x_prompt, x_sample, w_in, rw_mu_prev, rw_mu_next, rw_w0, rw_w_up, rw_a0, rw_a_up, rw_g_up, rw_k_k,
           rw_k_a, rw_r_k, rw_gn_g, rw_gn_b, q_norm, k_norm, w_proj_rwkv, w_proj_attn, w_out, ln1_g, ln1_b,
           w_ff1, w_ff2, ln2_g, ln2_b):
    def layer_params(l):
        w = w_in[l].astype(BF16)
        row = lambda a: a[l].reshape(1, -1)
        return dict(
            w_rw=w[:, :RW_COLS], w_qkv=w[:, RW_COLS:RW_COLS + QKV_COLS], w_gate=w[:, RW_COLS + QKV_COLS:],
            mu_prev=row(rw_mu_prev), mu_next=row(rw_mu_next),
            w0=rw_w0[l], w_up=_pad_rank(rw_w_up[l]), a0=rw_a0[l], a_up=_pad_rank(rw_a_up[l]),
            g_up=rw_g_up[l].astype(BF16), k_k=row(rw_k_k), k_a=row(rw_k_a), r_k=row(rw_r_k),
            gn_g=row(rw_gn_g), gn_b=row(rw_gn_b), q_norm=row(q_norm), k_norm=row(k_norm),
            w_prw=w_proj_rwkv[l].astype(BF16), w_patt=w_proj_attn[l].astype(BF16), w_out=w_out[l].astype(BF16),
            ln1_g=row(ln1_g), ln1_b=row(ln1_b), w_ff1=w_ff1[l].astype(BF16), w_ff2=w_ff2[l].astype(BF16),
            ln2_g=row(ln2_g), ln2_b=row(ln2_b))

    layers = [layer_params(l) for l in range(w_in.shape[0])]

    def trunk(x):
        for p in layers:
            x = _layer(x, p)
        return x

    return trunk(x_prompt), trunk(x_sample)
```
